```python
import jax, jax.numpy as jnp
from jax import lax
import numpy as np

D_MODEL = 1024
BATCH = 4
SEQ = 4096
DEPTH = 1

CTX_LEN = 256
GRID_W = 64
CONV_WIDTH = D_MODEL
CONV_K = 3
RET_HEADS = 8
RET_DV = D_MODEL // RET_HEADS
RET_DK = RET_DV // 2
RET_QK_WIDTH = RET_HEADS * RET_DK
RET_V_WIDTH = RET_HEADS * RET_DV
CHUNK = 128
ROPE_BASE = 10000.0
EPS = 1e-6
IN_WIDTHS = (CONV_WIDTH, CONV_WIDTH, CONV_WIDTH, CONV_WIDTH,
             RET_QK_WIDTH, RET_QK_WIDTH, RET_V_WIDTH, RET_V_WIDTH, D_MODEL, D_MODEL)
SPLIT_POINTS = tuple(int(s) for s in np.cumsum(IN_WIDTHS)[:-1])
IN_WIDTH = int(sum(IN_WIDTHS))

kernel_name = "hybrid_conv_retention_dit_block"


def rmsnorm(x, w):
    x32 = x.astype(jnp.float32)
    y = x32 * lax.rsqrt(jnp.mean(x32 * x32, axis=-1, keepdims=True) + EPS)
    return (y * w.astype(jnp.float32)).astype(x.dtype)


def dwconv_centred(u, w, b):
    L = u.shape[1]
    pad = CONV_K // 2
    up = jnp.pad(u, ((0, 0), (pad, pad), (0, 0)))
    return sum(up[:, j:j + L] * w[j] for j in range(CONV_K)) + b


def to_heads(t, d):
    B_, L, _ = t.shape
    return t.reshape(B_, L, RET_HEADS, d).transpose(0, 2, 1, 3)


def rope2d(t):
    L = t.shape[2]
    rows = L // GRID_W
    row = jnp.repeat(jnp.arange(rows), GRID_W).astype(jnp.float32)
    col = jnp.tile(jnp.arange(GRID_W), rows).astype(jnp.float32)
    nf = RET_DK // 4
    inv = ROPE_BASE ** (-jnp.arange(nf, dtype=jnp.float32) / nf)
    ang = jnp.concatenate([row[:, None] * inv, col[:, None] * inv], axis=-1)
    cos = jnp.cos(ang).astype(t.dtype)
    sin = jnp.sin(ang).astype(t.dtype)
    half = RET_DK // 2
    t1, t2 = t[..., :half], t[..., half:]
    return jnp.concatenate([t1 * cos - t2 * sin, t1 * sin + t2 * cos], axis=-1)


def retention_scan(q, k, v, log_gamma, s0):
    B_, H, L, dk = q.shape
    dv = v.shape[-1]
    n = L // CHUNK
    qc = q.astype(jnp.float32).reshape(B_, H, n, CHUNK, dk)
    kc = k.astype(jnp.float32).reshape(B_, H, n, CHUNK, dk)
    vc = v.astype(jnp.float32).reshape(B_, H, n, CHUNK, dv)
    idx = jnp.arange(CHUNK, dtype=jnp.float32)
    diff = idx[:, None] - idx[None, :]
    dmask = jnp.where(diff >= 0, jnp.exp(log_gamma[:, None, None] * jnp.maximum(diff, 0.0)), 0.0)
    scores = jnp.einsum('bhnid,bhnjd->bhnij', qc, kc) * dmask[None, :, None]
    inner = jnp.einsum('bhnij,bhnjv->bhniv', scores, vc)
    k_dec = jnp.exp(log_gamma[:, None] * (CHUNK - 1 - idx))
    kv = jnp.einsum('bhnjd,hj,bhnjv->bhndv', kc, k_dec, vc)
    chunk_decay = jnp.exp(log_gamma * CHUNK)[None, :, None, None]

    def step(s, kv_c):
        return chunk_decay * s + kv_c, s

    _, s_prev = lax.scan(step, s0.astype(jnp.float32), jnp.moveaxis(kv, 2, 0))
    s_prev = jnp.moveaxis(s_prev, 0, 2)
    q_dec = jnp.exp(log_gamma[:, None] * (idx + 1.0))
    cross = jnp.einsum('bhnid,hi,bhndv->bhniv', qc, q_dec, s_prev)
    return (inner + cross).reshape(B_, H, L, dv)


def bidir_retention(q, k, v, lg_f, lg_b, s0_f, s0_b):
    o_f = retention_scan(q, k, v, lg_f, s0_f)
    flip = lambda t: t[:, :, ::-1]
    o_b = retention_scan(flip(q), flip(k), flip(v), lg_b, s0_b)
    return o_f + flip(o_b)


def ctx_final_states(k, v, lg_f, lg_b):
    Lc = k.shape[2]
    m = jnp.arange(Lc, dtype=jnp.float32)
    k32 = k.astype(jnp.float32)
    v32 = v.astype(jnp.float32)
    dec_f = jnp.exp(lg_f[:, None] * (Lc - 1 - m))
    dec_b = jnp.exp(lg_b[:, None] * m)
    s_f = jnp.einsum('bhmd,hm,bhmv->bhdv', k32, dec_f, v32)
    s_b = jnp.einsum('bhmd,hm,bhmv->bhdv', k32, dec_b, v32)
    return s_f, s_b


def retention_groupnorm(ret, gn_w, dtype):
    mu = jnp.mean(ret, axis=-1, keepdims=True)
    var = jnp.mean(jnp.square(ret - mu), axis=-1, keepdims=True)
    rn = (ret - mu) * lax.rsqrt(var + EPS)
    B_, H, L, dv = rn.shape
    rn = rn.transpose(0, 2, 1, 3).reshape(B_, L, H * dv)
    return (rn * gn_w.astype(jnp.float32)).astype(dtype)


def split_heads(proj, rotary):
    h, bg, cg, za, q, k, v, zb, ga, gb = jnp.split(proj, SPLIT_POINTS, axis=-1)
    q = to_heads(q, RET_DK)
    k = to_heads(k, RET_DK) * (RET_DK ** -0.5)
    v = to_heads(v, RET_DV)
    if rotary:
        q, k = rope2d(q), rope2d(k)
    return (h, bg, cg, za, zb, ga, gb), (q, k, v)


def merge_branches(parts, ret, conv_w, conv_b, gn_w, w_a, w_b, w_out):
    h, bg, cg, za, zb, ga, gb = parts
    conv_out = dwconv_centred(cg * h, conv_w, conv_b)
    y_a = (jax.nn.silu(za) * bg * conv_out) @ w_a
    ret_n = retention_groupnorm(ret, gn_w, h.dtype)
    y_b = (jax.nn.silu(zb) * ret_n) @ w_b
    return (jax.nn.sigmoid(ga) * y_a + jax.nn.sigmoid(gb) * y_b) @ w_out


def setup_inputs(seed: int = 0) -> dict:
    key = jax.random.key(seed)
    ks = jax.random.split(key, 20)
    f32 = jnp.float32
    nrm = lambda k, shape, s: jax.random.normal(k, shape, f32) * s
    base_gamma = 1.0 - 2.0 ** (-5.0 - np.arange(RET_HEADS, dtype=np.float32))
    base_logit = jnp.asarray(np.log(base_gamma / (1.0 - base_gamma)), f32)
    decay_logit = base_logit[None, None, :] + nrm(ks[10], (DEPTH, 2, RET_HEADS), 0.1)
    return {
        "x": nrm(ks[0], (BATCH, SEQ, D_MODEL), 1.0),
        "c": nrm(ks[1], (BATCH, D_MODEL), 1.0),
        "ctx": nrm(ks[2], (BATCH, CTX_LEN, D_MODEL), 1.0),
        "c_ctx": nrm(ks[3], (D_MODEL,), 1.0),
        "norm_w": 1.0 + nrm(ks[4], (DEPTH, D_MODEL), 0.02),
        "ada_w": nrm(ks[5], (DEPTH, D_MODEL, 3 * D_MODEL), 0.5 * D_MODEL ** -0.5),
        "ada_b": nrm(ks[6], (DEPTH, 3 * D_MODEL), 0.02),
        "w_in": nrm(ks[7], (DEPTH, D_MODEL, IN_WIDTH), D_MODEL ** -0.5),
        "conv_w": nrm(ks[8], (DEPTH, CONV_K, CONV_WIDTH), CONV_K ** -0.5),
        "conv_b": nrm(ks[9], (DEPTH, CONV_WIDTH), 0.02),
        "decay_logit": decay_logit,
        "gn_w": 1.0 + nrm(ks[11], (DEPTH, RET_V_WIDTH), 0.02),
        "w_a": nrm(ks[12], (DEPTH, CONV_WIDTH, D_MODEL), CONV_WIDTH ** -0.5),
        "w_b": nrm(ks[13], (DEPTH, RET_V_WIDTH, D_MODEL), RET_V_WIDTH ** -0.5),
        "w_out": nrm(ks[14], (DEPTH, D_MODEL, D_MODEL), D_MODEL ** -0.5),
        "final_norm_w": 1.0 + nrm(ks[15], (D_MODEL,), 0.02),
    }


def reference(x, c, ctx, c_ctx, norm_w, ada_w, ada_b, w_in, conv_w, conv_b,
              decay_logit, gn_w, w_a, w_b, w_out, final_norm_w):
    for l in range(DEPTH):
        mod_x = jax.nn.silu(c) @ ada_w[l] + ada_b[l]
        sh_x, sc_x, g_x = jnp.split(mod_x[:, None, :], 3, axis=-1)
        mod_c = jax.nn.silu(c_ctx) @ ada_w[l] + ada_b[l]
        sh_c, sc_c, g_c = jnp.split(mod_c, 3, axis=-1)
        xm = rmsnorm(x, norm_w[l]) * (1.0 + sc_x) + sh_x
        cm = rmsnorm(ctx, norm_w[l]) * (1.0 + sc_c) + sh_c
        parts_x, (qx, kx, vx) = split_heads(xm @ w_in[l], rotary=True)
        parts_c, (qc, kc, vc) = split_heads(cm @ w_in[l], rotary=False)
        lg = jax.nn.log_sigmoid(decay_logit[l].astype(jnp.float32))
        s_f, s_b = ctx_final_states(kc, vc, lg[0], lg[1])
        ret_x = bidir_retention(qx, kx, vx, lg[0], lg[1], s_f, s_b)
        y_x = merge_branches(parts_x, ret_x, conv_w[l], conv_b[l], gn_w[l], w_a[l], w_b[l], w_out[l])
        if l < DEPTH - 1:
            zeros = jnp.zeros_like(s_f)
            ret_c = bidir_retention(qc, kc, vc, lg[0], lg[1], zeros, zeros)
            y_c = merge_branches(parts_c, ret_c, conv_w[l], conv_b[l], gn_w[l], w_a[l], w_b[l], w_out[l])
            ctx = ctx + g_c * y_c
        x = x + g_x * y_x
    return rmsnorm(x, final_norm_w)
```

```python
import functools

import numpy as np
import jax
import jax.numpy as jnp
from jax import lax
from jax.experimental import pallas as pl
from jax.experimental.pallas import tpu as pltpu

D_MODEL = 1024
HEADS = 8
DK = 64
DV = 128
CHUNK = 128
GRID_W = 64
ROPE_BASE = 10000.0
EPS = 1e-6
NPAIR = HEADS // 2
LANES = 128
HALO = 16
VMEM_LIMIT = 56 * 1024 * 1024

F32 = jnp.float32
BF16 = jnp.bfloat16

_OFF_H, _OFF_BG, _OFF_CG, _OFF_ZA = 0, 1024, 2048, 3072
_OFF_Q, _OFF_ZB, _OFF_GA, _OFF_GB = 4096, 4608, 5632, 6656
_W_MAIN = 7680


def _dot(a, b):
    return jnp.dot(a, b, preferred_element_type=F32)


def _log_sigmoid(x):
    return -(jnp.maximum(-x, 0.0) + jnp.log1p(jnp.exp(-jnp.abs(x))))


def _rms_scale(xv):
    return lax.rsqrt(jnp.mean(xv * xv, axis=-1, keepdims=True) + EPS)


def _mod_kernel(cc_ref, w_ref, b_ref, nw_ref, o_ref):
    j = pl.program_id(0)
    s = jax.nn.silu(cc_ref[...])
    val = jnp.dot(s, w_ref[...], preferred_element_type=F32,
                  precision=lax.Precision.HIGHEST) + b_ref[...]
    o_ref[...] = jnp.where(j == 1, nw_ref[...] * (1.0 + val), val)


def _modulation(cc, ada_w, ada_b, norm_w):
    return pl.pallas_call(
        _mod_kernel,
        grid=(3,),
        in_specs=[
            pl.BlockSpec((8, D_MODEL), lambda j: (0, 0)),
            pl.BlockSpec((D_MODEL, D_MODEL), lambda j: (0, j)),
            pl.BlockSpec((1, D_MODEL), lambda j: (0, j)),
            pl.BlockSpec((1, D_MODEL), lambda j: (0, 0)),
        ],
        out_specs=pl.BlockSpec((8, D_MODEL), lambda j: (0, j)),
        out_shape=jax.ShapeDtypeStruct((8, 3 * D_MODEL), F32),
        name="mod",
    )(cc, ada_w, ada_b, norm_w)


def _decay_kernel(dlh_ref, dlp_ref, dmask_ref, mkd_ref, qd_ref, cd_ref):
    c = float(CHUNK)
    ii = lax.broadcasted_iota(jnp.int32, (CHUNK, CHUNK), 0).astype(F32)
    jj = lax.broadcasted_iota(jnp.int32, (CHUNK, CHUNK), 1).astype(F32)
    diff = ii - jj
    lane = lax.broadcasted_iota(jnp.int32, (CHUNK, LANES), 1)
    row = lax.broadcasted_iota(jnp.int32, (CHUNK, LANES), 0).astype(F32)
    for h in range(HEADS):
        lgf = _log_sigmoid(dlh_ref[0, h, 0:1, :])
        lgb = _log_sigmoid(dlh_ref[1, h, 0:1, :])
        dmask_ref[h] = jnp.where(
            diff > 0.0, jnp.exp(lgf * jnp.maximum(diff, 0.0)),
            jnp.where(diff < 0.0, jnp.exp(lgb * jnp.maximum(-diff, 0.0)), 2.0))
        hm = (((lane >> 5) & 1) == (h % 2)).astype(F32)
        mkd_ref[0, h] = hm * jnp.exp(lgf * (c - 1.0 - row))
        mkd_ref[1, h] = hm * jnp.exp(lgb * row)
        cd_ref[0, h] = jnp.broadcast_to(jnp.exp(lgf * c), (8, LANES))
        cd_ref[1, h] = jnp.broadcast_to(jnp.exp(lgb * c), (8, LANES))
    for p in range(NPAIR):
        lgf = _log_sigmoid(dlp_ref[0, p, 0:1, :])
        lgb = _log_sigmoid(dlp_ref[1, p, 0:1, :])
        qd_ref[0, p] = jnp.exp(lgf * (row + 1.0))
        qd_ref[1, p] = jnp.exp(lgb * (c - row))


def _decay_tables(dlh, dlp):
    return pl.pallas_call(
        _decay_kernel,
        out_shape=(
            jax.ShapeDtypeStruct((HEADS, CHUNK, CHUNK), F32),
            jax.ShapeDtypeStruct((2, HEADS, CHUNK, LANES), F32),
            jax.ShapeDtypeStruct((2, NPAIR, CHUNK, LANES), F32),
            jax.ShapeDtypeStruct((2, HEADS, 8, LANES), F32),
        ),
        name="decay",
    )(dlh, dlp)


def _kv_kernel(x_ref, a_ref, sh_ref, w_ref, cos_ref, sin_ref, mkd_ref, cd_ref, sinit_ref,
               kx_ref, v_ref, sb_ref, kvf_ref, sfin_ref, *, rotary, nchunk):
    i = pl.program_id(1)
    x = x_ref[0]
    xm = (x * _rms_scale(x) * a_ref[0] + sh_ref[0]).astype(BF16)
    kv = _dot(xm, w_ref[...])
    v_ref[0] = kv[:, HEADS * DK:].astype(BF16)

    @pl.when(i == 0)
    def _():
        sfin_ref[0] = sinit_ref[0]

    lane = lax.broadcasted_iota(jnp.int32, (1, LANES), 1)
    krs = []
    for p in range(NPAIR):
        kp = kv[:, p * LANES:(p + 1) * LANES]
        if rotary:
            kp = kp * cos_ref[...] + pltpu.roll(kp, LANES // 2, 1) * sin_ref[...]
        krs.append(kp)
        for e in range(2):
            h = 2 * p + e
            keep = ((lane >> 5) & 1) == e
            kx_ref[0, :, h * LANES:(h + 1) * LANES] = jnp.where(keep, kp, 0.0).astype(BF16)

    for c in reversed(range(nchunk)):
        rows = slice(c * CHUNK, (c + 1) * CHUNK)
        for h in range(HEADS):
            kc = krs[h // 2][rows]
            vc = kv[rows, HEADS * DK + h * DV:HEADS * DK + (h + 1) * DV].astype(BF16)
            kd = jnp.concatenate([kc * mkd_ref[0, h], kc * mkd_ref[1, h]], axis=1)
            inc = _dot(kd.T.astype(BF16), vc)
            sb_ref[0, c, h] = sfin_ref[0, h].astype(BF16)
            kvf_ref[0, c, h] = inc[:LANES]
            sfin_ref[0, h] = cd_ref[1, h, 0:1, :] * sfin_ref[0, h] + inc[LANES:]


def _kv_states(xs, a, sh, w_kv, cos, sin, mkd, cd, sinit, *, rotary, tile):
    b, l, _ = xs.shape
    nt = l // tile
    nchunk = tile // CHUNK
    nc = l // CHUNK
    rev = lambda bi, i: (bi, nt - 1 - i, 0)
    return pl.pallas_call(
        functools.partial(_kv_kernel, rotary=rotary, nchunk=nchunk),
        grid=(b, nt),
        in_specs=[
            pl.BlockSpec((1, tile, D_MODEL), rev),
            pl.BlockSpec((1, 1, D_MODEL), lambda bi, i: (bi, 0, 0)),
            pl.BlockSpec((1, 1, D_MODEL), lambda bi, i: (bi, 0, 0)),
            pl.BlockSpec((D_MODEL, HEADS * (DK + DV)), lambda bi, i: (0, 0)),
            pl.BlockSpec((tile, LANES), lambda bi, i: (nt - 1 - i, 0)),
            pl.BlockSpec((tile, LANES), lambda bi, i: (nt - 1 - i, 0)),
            pl.BlockSpec((2, HEADS, CHUNK, LANES), lambda bi, i: (0, 0, 0, 0)),
            pl.BlockSpec((2, HEADS, 8, LANES), lambda bi, i: (0, 0, 0, 0)),
            pl.BlockSpec((1, HEADS, LANES, DV), lambda bi, i: (bi, 0, 0, 0)),
        ],
        out_specs=(
            pl.BlockSpec((1, tile, HEADS * LANES), rev),
            pl.BlockSpec((1, tile, HEADS * DV), rev),
            pl.BlockSpec((1, nchunk, HEADS, LANES, DV), lambda bi, i: (bi, nt - 1 - i, 0, 0, 0)),
            pl.BlockSpec((1, nchunk, HEADS, LANES, DV), lambda bi, i: (bi, nt - 1 - i, 0, 0, 0)),
            pl.BlockSpec((1, HEADS, LANES, DV), lambda bi, i: (bi, 0, 0, 0)),
        ),
        out_shape=(
            jax.ShapeDtypeStruct((b, l, HEADS * LANES), BF16),
            jax.ShapeDtypeStruct((b, l, HEADS * DV), BF16),
            jax.ShapeDtypeStruct((b, nc, HEADS, LANES, DV), BF16),
            jax.ShapeDtypeStruct((b, nc, HEADS, LANES, DV), F32),
            jax.ShapeDtypeStruct((b, HEADS, LANES, DV), F32),
        ),
        compiler_params=pltpu.CompilerParams(
            dimension_semantics=("arbitrary", "arbitrary"), vmem_limit_bytes=VMEM_LIMIT),
        name="kv_rot" if rotary else "kv_ctx",
    )(xs, a, sh, w_kv, cos, sin, mkd, cd, sinit)


def _main_kernel(x_ref, xp_ref, xn_ref, a_ref, sh_ref, g_ref, fnw_ref,
                 wm_ref, wa_ref, wb_ref, wo_ref, cw_ref, cb_ref, gnw_ref,
                 kx_ref, v_ref, sb_ref, kvf_ref, kvfc_ref, cd_ref,
                 cos_ref, sin_ref, dmask_ref, qd_ref,
                 o_ref,
                 sf_ref, xme_ref, ue_ref, ain_ref, bin_ref, m_ref, *, tile, nt, cblk):
    i = pl.program_id(1)
    nchunk = tile // CHUNK

    def modulate(xv):
        return xv * _rms_scale(xv) * a_ref[0] + sh_ref[0]

    x = x_ref[0]
    xme_ref[0:HALO] = jnp.where(i > 0, modulate(xp_ref[0]), 0.0).astype(BF16)
    xme_ref[HALO:HALO + tile] = modulate(x).astype(BF16)
    xme_ref[HALO + tile:] = jnp.where(i < nt - 1, modulate(xn_ref[0]), 0.0).astype(BF16)

    @pl.when(i == 0)
    def _():
        for h in range(HEADS):
            sf_ref[h] = cd_ref[0, h, 0:1, :] * kvfc_ref[0, 0, h] + kvfc_ref[0, 1, h]

    for j in range(D_MODEL // cblk):
        cs = slice(j * cblk, (j + 1) * cblk)
        xe = xme_ref[...]
        hh = _dot(xe, wm_ref[:, _OFF_H + j * cblk:_OFF_H + (j + 1) * cblk])
        cg = _dot(xe, wm_ref[:, _OFF_CG + j * cblk:_OFF_CG + (j + 1) * cblk])
        ue_ref[...] = hh * cg
        conv = (cw_ref[0:1, cs] * ue_ref[HALO - 1:HALO - 1 + tile]
                + cw_ref[1:2, cs] * ue_ref[HALO:HALO + tile]
                + cw_ref[2:3, cs] * ue_ref[HALO + 1:HALO + 1 + tile]
                + cb_ref[:, cs])
        xmv = xme_ref[HALO:HALO + tile]
        bg = _dot(xmv, wm_ref[:, _OFF_BG + j * cblk:_OFF_BG + (j + 1) * cblk])
        za = _dot(xmv, wm_ref[:, _OFF_ZA + j * cblk:_OFF_ZA + (j + 1) * cblk])
        ain_ref[:, cs] = (jax.nn.silu(za) * bg * conv).astype(BF16)
    ya = _dot(ain_ref[...], wa_ref[...])

    xmv = xme_ref[HALO:HALO + tile]
    q = _dot(xmv, wm_ref[:, _OFF_Q:_OFF_Q + HEADS * DK])
    zb = _dot(xmv, wm_ref[:, _OFF_ZB:_OFF_ZB + HEADS * DV])
    qr = []
    for p in range(NPAIR):
        qp = q[:, p * LANES:(p + 1) * LANES]
        qr.append(qp * cos_ref[...] + pltpu.roll(qp, LANES // 2, 1) * sin_ref[...])
    for c in range(nchunk):
        rows = slice(c * CHUNK, (c + 1) * CHUNK)
        for p in range(NPAIR):
            qp = qr[p][rows]
            qpb = qp.astype(BF16)
            qf = (qp * qd_ref[0, p]).astype(BF16)
            qb = (qp * qd_ref[1, p]).astype(BF16)
            for e in range(2):
                h = 2 * p + e
                hs = slice(h * DV, (h + 1) * DV)
                s = lax.dot_general(qpb, kx_ref[0, rows, hs], (((1,), (1,)), ((), ())),
                                    preferred_element_type=F32) * dmask_ref[h]
                o = (_dot(s.astype(BF16), v_ref[0, rows, hs])
                     + _dot(qf, sf_ref[h].astype(BF16))
                     + _dot(qb, sb_ref[0, c, h]))
                mu = jnp.mean(o, axis=-1, keepdims=True)
                d = o - mu
                var = jnp.mean(d * d, axis=-1, keepdims=True)
                rn = d * lax.rsqrt(var + EPS) * gnw_ref[:, hs]
                bin_ref[rows, hs] = (jax.nn.silu(zb[rows, hs]) * rn).astype(BF16)
        for h in range(HEADS):
            sf_ref[h] = cd_ref[0, h, 0:1, :] * sf_ref[h] + kvf_ref[0, c, h]
    yb = _dot(bin_ref[...], wb_ref[...])

    for j in range(D_MODEL // cblk):
        cs = slice(j * cblk, (j + 1) * cblk)
        ga = _dot(xmv, wm_ref[:, _OFF_GA + j * cblk:_OFF_GA + (j + 1) * cblk])
        gb = _dot(xmv, wm_ref[:, _OFF_GB + j * cblk:_OFF_GB + (j + 1) * cblk])
        m_ref[:, cs] = (jax.nn.sigmoid(ga) * ya[:, cs] + jax.nn.sigmoid(gb) * yb[:, cs]).astype(BF16)
    y = _dot(m_ref[...], wo_ref[...])
    z = x + g_ref[0] * y
    o_ref[0] = z * _rms_scale(z) * fnw_ref[...]


def _main(x, a, sh, g, fnw, w_main, w_a, w_b, w_out, conv_w, conv_b, gn_w,
          kx, v, sb, kvf, kvf_ctx, cd, cos, sin, dmask, qd, *, tile, cblk):
    b, l, _ = x.shape
    nt = l // tile
    nchunk = tile // CHUNK
    hb = tile // HALO
    nhb = l // HALO
    const = lambda *shape: pl.BlockSpec(shape, lambda bi, i: (0,) * len(shape),
                                        pipeline_mode=pl.Buffered(1))
    per_b = pl.BlockSpec((1, 1, D_MODEL), lambda bi, i: (bi, 0, 0))
    tile_rows = lambda w: pl.BlockSpec((1, tile, w), lambda bi, i: (bi, i, 0))
    chunk_state = pl.BlockSpec((1, nchunk, HEADS, LANES, DV), lambda bi, i: (bi, i, 0, 0, 0))
    return pl.pallas_call(
        functools.partial(_main_kernel, tile=tile, nt=nt, cblk=cblk),
        grid=(b, nt),
        in_specs=[
            tile_rows(D_MODEL),
            pl.BlockSpec((1, HALO, D_MODEL), lambda bi, i: (bi, jnp.maximum(i * hb - 1, 0), 0)),
            pl.BlockSpec((1, HALO, D_MODEL), lambda bi, i: (bi, jnp.minimum((i + 1) * hb, nhb - 1), 0)),
            per_b, per_b, per_b,
            const(1, D_MODEL),
            const(D_MODEL, _W_MAIN),
            const(D_MODEL, D_MODEL), const(D_MODEL, D_MODEL), const(D_MODEL, D_MODEL),
            const(3, D_MODEL), const(1, D_MODEL), const(1, D_MODEL),
            tile_rows(HEADS * LANES), tile_rows(HEADS * DV),
            chunk_state, chunk_state,
            pl.BlockSpec((1, 2, HEADS, LANES, DV), lambda bi, i: (bi, 0, 0, 0, 0)),
            const(2, HEADS, 8, LANES),
            pl.BlockSpec((tile, LANES), lambda bi, i: (i, 0)),
            pl.BlockSpec((tile, LANES), lambda bi, i: (i, 0)),
            const(HEADS, CHUNK, CHUNK),
            const(2, NPAIR, CHUNK, LANES),
        ],
        out_specs=tile_rows(D_MODEL),
        out_shape=jax.ShapeDtypeStruct((b, l, D_MODEL), F32),
        scratch_shapes=[
            pltpu.VMEM((HEADS, LANES, DV), F32),
            pltpu.VMEM((tile + 2 * HALO, D_MODEL), BF16),
            pltpu.VMEM((tile + 2 * HALO, cblk), F32),
            pltpu.VMEM((tile, D_MODEL), BF16),
            pltpu.VMEM((tile, D_MODEL), BF16),
            pltpu.VMEM((tile, D_MODEL), BF16),
        ],
        compiler_params=pltpu.CompilerParams(
            dimension_semantics=("arbitrary", "arbitrary"), vmem_limit_bytes=VMEM_LIMIT),
        name="main",
    )(x, x, x, a, sh, g, fnw, w_main, w_a, w_b, w_out, conv_w, conv_b, gn_w,
      kx, v, sb, kvf, kvf_ctx, cd, cos, sin, dmask, qd)


def _qk_perm():
    perm = np.zeros(HEADS * DK, np.int32)
    for gp in range(NPAIR):
        for ln in range(LANES):
            s = ln // 32
            perm[gp * LANES + ln] = (2 * gp + (s % 2)) * DK + (s // 2) * 32 + (ln % 32)
    return perm


def _rope_tables(l):
    pos = np.arange(l)
    row = (pos // GRID_W).astype(np.float64)
    col = (pos % GRID_W).astype(np.float64)
    nf = DK // 4
    inv = ROPE_BASE ** (-np.arange(nf, dtype=np.float64) / nf)
    ang = np.concatenate([row[:, None] * inv, col[:, None] * inv], axis=-1)
    cos = np.tile(np.cos(ang), (1, 4))
    sin = np.tile(np.sin(ang), (1, 4))
    sin[:, :LANES // 2] *= -1.0
    return jnp.asarray(cos, F32), jnp.asarray(sin, F32)


def kernel(x, c, ctx, c_ctx, norm_w, ada_w, ada_b, w_in, conv_w, conv_b, decay_logit, gn_w,
           w_a, w_b, w_out, final_norm_w):
    b, l, _ = x.shape
    lc = ctx.shape[1]
    kv_tile = 256
    main_tile = 256
    cblk = 256

    cc = jnp.zeros((8, D_MODEL), F32).at[:b].set(c).at[b].set(c_ctx)
    mod = _modulation(cc, ada_w[0], ada_b[0].reshape(1, -1), norm_w[0].reshape(1, -1))
    sh, a, g = mod[:, :D_MODEL], mod[:, D_MODEL:2 * D_MODEL], mod[:, 2 * D_MODEL:]
    sh_x, a_x, g_x = (t[:b].reshape(b, 1, D_MODEL) for t in (sh, a, g))
    sh_c = jnp.broadcast_to(sh[b].reshape(1, 1, D_MODEL), (b, 1, D_MODEL))
    a_c = jnp.broadcast_to(a[b].reshape(1, 1, D_MODEL), (b, 1, D_MODEL))

    dl = decay_logit[0].astype(F32)
    dlh = jnp.broadcast_to(dl[:, :, None, None], (2, HEADS, 8, LANES))
    lane_head = np.array([[2 * p + ((ln // 32) % 2) for ln in range(LANES)] for p in range(NPAIR)])
    dlp = jnp.broadcast_to(dl[:, lane_head][:, :, None, :], (2, NPAIR, 8, LANES))
    dmask, mkd, qd, cd = _decay_tables(dlh, dlp)

    w = w_in[0]
    perm = _qk_perm()
    wq = w[:, 4096:4608][:, perm]
    wk = w[:, 4608:5120][:, perm] * (DK ** -0.5)
    w_kv = jnp.concatenate([wk, w[:, 5120:6144]], axis=1).astype(BF16)
    w_main = jnp.concatenate([w[:, :4096], wq, w[:, 6144:]], axis=1).astype(BF16)

    cos, sin = _rope_tables(l)
    zeros_state = jnp.zeros((b, HEADS, LANES, DV), F32)

    _, _, _, kvf_ctx, sb0 = _kv_states(ctx, a_c, sh_c, w_kv, cos[:lc], sin[:lc], mkd, cd,
                                       zeros_state, rotary=False, tile=lc)
    kx, v, sb, kvf, _ = _kv_states(x, a_x, sh_x, w_kv, cos, sin, mkd, cd, sb0,
                                   rotary=True, tile=kv_tile)

    return _main(x, a_x, sh_x, g_x, final_norm_w.reshape(1, -1), w_main,
                 w_a[0].astype(BF16), w_b[0].astype(BF16), w_out[0].astype(BF16),
                 conv_w[0], conv_b[0].reshape(1, -1), gn_w[0].reshape(1, -1),
                 kx, v, sb, kvf, kvf_ctx, cd, cos, sin, dmask, qd,
                 tile=main_tile, cblk=cblk)
```

```python
import functools

import numpy as np
import jax
import jax.numpy as jnp
from jax import lax
from jax.experimental import pallas as pl
from jax.experimental.pallas import tpu as pltpu

D_MODEL = 1024
HEADS = 8
DK = 64
DV = 128
CHUNK = 128
GRID_W = 64
ROPE_BASE = 10000.0
EPS = 1e-6
NPAIR = HEADS // 2
LANES = 128
HALO = 16
VMEM_LIMIT = 56 * 1024 * 1024

F32 = jnp.float32
BF16 = jnp.bfloat16

_Q_COL, _K_COL, _V_COL = 4096, 4608, 5120
_V_BLOCK = _V_COL // (HEADS * DV)
_A_WIDTH = 4096
_B_WIDTH = 3072
_B_BLOCK = 6144 // _B_WIDTH
_OFF_H, _OFF_BG, _OFF_CG, _OFF_ZA = 0, 1024, 2048, 3072
_OFF_ZB, _OFF_GA, _OFF_GB = 0, 1024, 2048


def _dot(a, b):
    return jnp.dot(a, b, preferred_element_type=F32)


def _log_sigmoid(x):
    return -(jnp.maximum(-x, 0.0) + jnp.log1p(jnp.exp(-jnp.abs(x))))


def _rms_scale(xv):
    return lax.rsqrt(jnp.mean(xv * xv, axis=-1, keepdims=True) + EPS)


def _mod_kernel(cc_ref, w_ref, b_ref, nw_ref, o_ref):
    j = pl.program_id(0)
    s = jax.nn.silu(cc_ref[...])
    val = jnp.dot(s, w_ref[...], preferred_element_type=F32,
                  precision=lax.Precision.HIGHEST) + b_ref[...]
    o_ref[...] = jnp.where(j == 1, nw_ref[...] * (1.0 + val), val)


def _modulation(cc, ada_w, ada_b, norm_w):
    return pl.pallas_call(
        _mod_kernel,
        grid=(3,),
        in_specs=[
            pl.BlockSpec((8, D_MODEL), lambda j: (0, 0)),
            pl.BlockSpec((D_MODEL, D_MODEL), lambda j: (0, j)),
            pl.BlockSpec((1, D_MODEL), lambda j: (0, j)),
            pl.BlockSpec((1, D_MODEL), lambda j: (0, 0)),
        ],
        out_specs=pl.BlockSpec((8, D_MODEL), lambda j: (0, j)),
        out_shape=jax.ShapeDtypeStruct((8, 3 * D_MODEL), F32),
        name="mod",
    )(cc, ada_w, ada_b, norm_w)


def _decay_kernel(dlh_ref, dlp_ref, dmask_ref, mkd_ref, qd_ref, cd_ref):
    c = float(CHUNK)
    ii = lax.broadcasted_iota(jnp.int32, (CHUNK, CHUNK), 0).astype(F32)
    jj = lax.broadcasted_iota(jnp.int32, (CHUNK, CHUNK), 1).astype(F32)
    diff = ii - jj
    lane = lax.broadcasted_iota(jnp.int32, (CHUNK, LANES), 1)
    row = lax.broadcasted_iota(jnp.int32, (CHUNK, LANES), 0).astype(F32)
    for h in range(HEADS):
        lgf = _log_sigmoid(dlh_ref[0, h, 0:1, :])
        lgb = _log_sigmoid(dlh_ref[1, h, 0:1, :])
        dmask_ref[h] = jnp.where(
            diff > 0.0, jnp.exp(lgf * jnp.maximum(diff, 0.0)),
            jnp.where(diff < 0.0, jnp.exp(lgb * jnp.maximum(-diff, 0.0)), 2.0))
        hm = (((lane >> 5) & 1) == (h % 2)).astype(F32)
        mkd_ref[0, h] = hm * jnp.exp(lgf * (c - 1.0 - row))
        mkd_ref[1, h] = hm * jnp.exp(lgb * row)
        cd_ref[0, h] = jnp.broadcast_to(jnp.exp(lgf * c), (8, LANES))
        cd_ref[1, h] = jnp.broadcast_to(jnp.exp(lgb * c), (8, LANES))
    for p in range(NPAIR):
        lgf = _log_sigmoid(dlp_ref[0, p, 0:1, :])
        lgb = _log_sigmoid(dlp_ref[1, p, 0:1, :])
        qd_ref[0, p] = jnp.exp(lgf * (row + 1.0))
        qd_ref[1, p] = jnp.exp(lgb * (c - row))


def _decay_tables(dlh, dlp):
    return pl.pallas_call(
        _decay_kernel,
        out_shape=(
            jax.ShapeDtypeStruct((HEADS, CHUNK, CHUNK), F32),
            jax.ShapeDtypeStruct((2, HEADS, CHUNK, LANES), F32),
            jax.ShapeDtypeStruct((2, NPAIR, CHUNK, LANES), F32),
            jax.ShapeDtypeStruct((2, HEADS, 8, LANES), F32),
        ),
        name="decay",
    )(dlh, dlp)


def _kv_kernel(x_ref, a_ref, sh_ref, wk_ref, wv_ref, cos_ref, sin_ref, mkd_ref, cd_ref, sinit_ref,
               kx_ref, v_ref, sb_ref, kvf_ref, sfin_ref, *, rotary, nchunk):
    i = pl.program_id(1)
    x = x_ref[0]
    xm = (x * _rms_scale(x) * a_ref[0] + sh_ref[0]).astype(BF16)
    k = _dot(xm, wk_ref[...])
    v = _dot(xm, wv_ref[...]).astype(BF16)
    v_ref[0] = v

    @pl.when(i == 0)
    def _():
        sfin_ref[0] = sinit_ref[0]

    lane = lax.broadcasted_iota(jnp.int32, (1, LANES), 1)
    krs = []
    for p in range(NPAIR):
        kp = k[:, p * LANES:(p + 1) * LANES]
        if rotary:
            kp = kp * cos_ref[...] + pltpu.roll(kp, LANES // 2, 1) * sin_ref[...]
        krs.append(kp)
        for e in range(2):
            h = 2 * p + e
            keep = ((lane >> 5) & 1) == e
            kx_ref[0, :, h * LANES:(h + 1) * LANES] = jnp.where(keep, kp, 0.0).astype(BF16)

    for c in reversed(range(nchunk)):
        rows = slice(c * CHUNK, (c + 1) * CHUNK)
        for h in range(HEADS):
            kc = krs[h // 2][rows]
            vc = v[rows, h * DV:(h + 1) * DV]
            kd = jnp.concatenate([kc * mkd_ref[0, h], kc * mkd_ref[1, h]], axis=1)
            inc = _dot(kd.T.astype(BF16), vc)
            sb_ref[0, c, h] = sfin_ref[0, h].astype(BF16)
            kvf_ref[0, c, h] = inc[:LANES]
            sfin_ref[0, h] = cd_ref[1, h, 0:1, :] * sfin_ref[0, h] + inc[LANES:]


def _kv_states(xs, a, sh, wk, w_all, cos, sin, mkd, cd, sinit, *, rotary, tile):
    b, l, _ = xs.shape
    nt = l // tile
    nchunk = tile // CHUNK
    nc = l // CHUNK
    rev = lambda bi, i: (bi, nt - 1 - i, 0)
    return pl.pallas_call(
        functools.partial(_kv_kernel, rotary=rotary, nchunk=nchunk),
        grid=(b, nt),
        in_specs=[
            pl.BlockSpec((1, tile, D_MODEL), rev),
            pl.BlockSpec((1, 1, D_MODEL), lambda bi, i: (bi, 0, 0)),
            pl.BlockSpec((1, 1, D_MODEL), lambda bi, i: (bi, 0, 0)),
            pl.BlockSpec((D_MODEL, HEADS * DK), lambda bi, i: (0, 0)),
            pl.BlockSpec((D_MODEL, HEADS * DV), lambda bi, i: (0, _V_BLOCK)),
            pl.BlockSpec((tile, LANES), lambda bi, i: (nt - 1 - i, 0)),
            pl.BlockSpec((tile, LANES), lambda bi, i: (nt - 1 - i, 0)),
            pl.BlockSpec((2, HEADS, CHUNK, LANES), lambda bi, i: (0, 0, 0, 0)),
            pl.BlockSpec((2, HEADS, 8, LANES), lambda bi, i: (0, 0, 0, 0)),
            pl.BlockSpec((1, HEADS, LANES, DV), lambda bi, i: (bi, 0, 0, 0)),
        ],
        out_specs=(
            pl.BlockSpec((1, tile, HEADS * LANES), rev),
            pl.BlockSpec((1, tile, HEADS * DV), rev),
            pl.BlockSpec((1, nchunk, HEADS, LANES, DV), lambda bi, i: (bi, nt - 1 - i, 0, 0, 0)),
            pl.BlockSpec((1, nchunk, HEADS, LANES, DV), lambda bi, i: (bi, nt - 1 - i, 0, 0, 0)),
            pl.BlockSpec((1, HEADS, LANES, DV), lambda bi, i: (bi, 0, 0, 0)),
        ),
        out_shape=(
            jax.ShapeDtypeStruct((b, l, HEADS * LANES), BF16),
            jax.ShapeDtypeStruct((b, l, HEADS * DV), BF16),
            jax.ShapeDtypeStruct((b, nc, HEADS, LANES, DV), BF16),
            jax.ShapeDtypeStruct((b, nc, HEADS, LANES, DV), F32),
            jax.ShapeDtypeStruct((b, HEADS, LANES, DV), F32),
        ),
        compiler_params=pltpu.CompilerParams(
            dimension_semantics=("arbitrary", "arbitrary"), vmem_limit_bytes=VMEM_LIMIT),
        name="kv_rot" if rotary else "kv_ctx",
    )(xs, a, sh, wk, w_all, cos, sin, mkd, cd, sinit)


def _main_kernel(x_ref, xp_ref, xn_ref, a_ref, sh_ref, g_ref, fnw_ref,
                 wa4_ref, wq_ref, wb3_ref, wa_ref, wb_ref, wo_ref, cw_ref, cb_ref, gnw_ref,
                 kx_ref, v_ref, sb_ref, kvf_ref, kvfc_ref, cd_ref,
                 cos_ref, sin_ref, dmask_ref, qd_ref,
                 o_ref,
                 sf_ref, xme_ref, ue_ref, ain_ref, bin_ref, m_ref, qr_ref, szb_ref, pa_ref, sgb_ref,
                 s_ref, *, tile, nt, cblk):
    i = pl.program_id(1)
    nchunk = tile // CHUNK
    nblk = D_MODEL // cblk

    def modulate(xv):
        return xv * _rms_scale(xv) * a_ref[0] + sh_ref[0]

    x = x_ref[0]
    xme_ref[0:HALO] = jnp.where(i > 0, modulate(xp_ref[0]), 0.0).astype(BF16)
    xme_ref[HALO:HALO + tile] = modulate(x).astype(BF16)
    xme_ref[HALO + tile:] = jnp.where(i < nt - 1, modulate(xn_ref[0]), 0.0).astype(BF16)

    @pl.when(i == 0)
    def _():
        for h in range(HEADS):
            sf_ref[h] = cd_ref[0, h, 0:1, :] * kvfc_ref[0, 0, h] + kvfc_ref[0, 1, h]

    xmv = xme_ref[HALO:HALO + tile]

    q = _dot(xmv, wq_ref[...])
    for p in range(NPAIR):
        qp = q[:, p * LANES:(p + 1) * LANES]
        qr_ref[:, p * LANES:(p + 1) * LANES] = (
            qp * cos_ref[...] + pltpu.roll(qp, LANES // 2, 1) * sin_ref[...])
    szb_ref[...] = jax.nn.silu(_dot(xmv, wb3_ref[:, _OFF_ZB:_OFF_ZB + HEADS * DV]))

    def conv_block(j):
        cs = slice(j * cblk, (j + 1) * cblk)
        xe = xme_ref[...]
        hh = _dot(xe, wa4_ref[:, _OFF_H + j * cblk:_OFF_H + (j + 1) * cblk])
        cg = _dot(xe, wa4_ref[:, _OFF_CG + j * cblk:_OFF_CG + (j + 1) * cblk])
        ue_ref[...] = hh * cg
        conv = (cw_ref[0:1, cs] * ue_ref[HALO - 1:HALO - 1 + tile]
                + cw_ref[1:2, cs] * ue_ref[HALO:HALO + tile]
                + cw_ref[2:3, cs] * ue_ref[HALO + 1:HALO + 1 + tile]
                + cb_ref[:, cs])
        bg = _dot(xmv, wa4_ref[:, _OFF_BG + j * cblk:_OFF_BG + (j + 1) * cblk])
        za = _dot(xmv, wa4_ref[:, _OFF_ZA + j * cblk:_OFF_ZA + (j + 1) * cblk])
        ain_ref[:, cs] = (jax.nn.silu(za) * bg * conv).astype(BF16)

    def scores(c, p, slot):
        rows = slice(c * CHUNK, (c + 1) * CHUNK)
        qpb = qr_ref[rows, p * LANES:(p + 1) * LANES].astype(BF16)
        for e in range(2):
            h = 2 * p + e
            s = lax.dot_general(qpb, kx_ref[0, rows, h * DV:(h + 1) * DV], (((1,), (1,)), ((), ())),
                                preferred_element_type=F32) * dmask_ref[h]
            s_ref[2 * slot + e] = s.astype(BF16)

    def outputs(c, p, slot):
        rows = slice(c * CHUNK, (c + 1) * CHUNK)
        qp = qr_ref[rows, p * LANES:(p + 1) * LANES]
        qfb = jnp.concatenate([(qp * qd_ref[0, p]).astype(BF16),
                               (qp * qd_ref[1, p]).astype(BF16)], axis=1)
        for e in range(2):
            h = 2 * p + e
            hs = slice(h * DV, (h + 1) * DV)
            states = jnp.concatenate([sf_ref[h].astype(BF16), sb_ref[0, c, h]], axis=0)
            o = _dot(s_ref[2 * slot + e], v_ref[0, rows, hs]) + _dot(qfb, states)
            mu = jnp.mean(o, axis=-1, keepdims=True)
            d = o - mu
            var = jnp.mean(d * d, axis=-1, keepdims=True)
            rn = d * lax.rsqrt(var + EPS) * gnw_ref[:, hs]
            bin_ref[rows, hs] = (szb_ref[rows, hs] * rn).astype(BF16)
        if p == NPAIR - 1:
            for h in range(HEADS):
                sf_ref[h] = cd_ref[0, h, 0:1, :] * sf_ref[h] + kvf_ref[0, c, h]

    units = [(c, p) for c in range(nchunk) for p in range(NPAIR)]
    per_blk = len(units) // nblk
    for j in range(nblk):
        group = units[j * per_blk:(j + 1) * per_blk]
        for slot, (c, p) in enumerate(group):
            scores(c, p, slot)
        conv_block(j)
        for slot, (c, p) in enumerate(group):
            outputs(c, p, slot)

    for j in range(nblk):
        cs = slice(j * cblk, (j + 1) * cblk)
        pa_ref[:, cs] = jax.nn.sigmoid(
            _dot(xmv, wb3_ref[:, _OFF_GA + j * cblk:_OFF_GA + (j + 1) * cblk]))
        sgb_ref[:, cs] = jax.nn.sigmoid(
            _dot(xmv, wb3_ref[:, _OFF_GB + j * cblk:_OFF_GB + (j + 1) * cblk]))
    pa_ref[...] = pa_ref[...] * _dot(ain_ref[...], wa_ref[...])

    halves = [slice(r * (tile // 2), (r + 1) * (tile // 2)) for r in range(2)]
    for rs in halves:
        yb = _dot(bin_ref[rs], wb_ref[...])
        m_ref[rs] = (pa_ref[rs] + sgb_ref[rs] * yb).astype(BF16)
    for rs in halves:
        z = x_ref[0, rs] + g_ref[0] * _dot(m_ref[rs], wo_ref[...])
        o_ref[0, rs] = z * _rms_scale(z) * fnw_ref[...]


def _main(x, a, sh, g, fnw, w_all, wq, w_a, w_b, w_out, conv_w, conv_b, gn_w,
          kx, v, sb, kvf, kvf_ctx, cd, cos, sin, dmask, qd, *, tile, cblk):
    b, l, _ = x.shape
    nt = l // tile
    nchunk = tile // CHUNK
    hb = tile // HALO
    nhb = l // HALO
    const = lambda *shape: pl.BlockSpec(shape, lambda bi, i: (0,) * len(shape),
                                        pipeline_mode=pl.Buffered(1))
    per_b = pl.BlockSpec((1, 1, D_MODEL), lambda bi, i: (bi, 0, 0))
    tile_rows = lambda w: pl.BlockSpec((1, tile, w), lambda bi, i: (bi, i, 0))
    chunk_state = pl.BlockSpec((1, nchunk, HEADS, LANES, DV), lambda bi, i: (bi, i, 0, 0, 0))
    return pl.pallas_call(
        functools.partial(_main_kernel, tile=tile, nt=nt, cblk=cblk),
        grid=(b, nt),
        in_specs=[
            tile_rows(D_MODEL),
            pl.BlockSpec((1, HALO, D_MODEL), lambda bi, i: (bi, jnp.maximum(i * hb - 1, 0), 0)),
            pl.BlockSpec((1, HALO, D_MODEL), lambda bi, i: (bi, jnp.minimum((i + 1) * hb, nhb - 1), 0)),
            per_b, per_b, per_b,
            const(1, D_MODEL),
            pl.BlockSpec((D_MODEL, _A_WIDTH), lambda bi, i: (0, 0), pipeline_mode=pl.Buffered(1)),
            const(D_MODEL, HEADS * DK),
            pl.BlockSpec((D_MODEL, _B_WIDTH), lambda bi, i: (0, _B_BLOCK), pipeline_mode=pl.Buffered(1)),
            const(D_MODEL, D_MODEL), const(D_MODEL, D_MODEL), const(D_MODEL, D_MODEL),
            const(3, D_MODEL), const(1, D_MODEL), const(1, D_MODEL),
            tile_rows(HEADS * LANES), tile_rows(HEADS * DV),
            chunk_state, chunk_state,
            pl.BlockSpec((1, 2, HEADS, LANES, DV), lambda bi, i: (bi, 0, 0, 0, 0)),
            const(2, HEADS, 8, LANES),
            pl.BlockSpec((tile, LANES), lambda bi, i: (i, 0)),
            pl.BlockSpec((tile, LANES), lambda bi, i: (i, 0)),
            const(HEADS, CHUNK, CHUNK),
            const(2, NPAIR, CHUNK, LANES),
        ],
        out_specs=tile_rows(D_MODEL),
        out_shape=jax.ShapeDtypeStruct((b, l, D_MODEL), F32),
        scratch_shapes=[
            pltpu.VMEM((HEADS, LANES, DV), F32),
            pltpu.VMEM((tile + 2 * HALO, D_MODEL), BF16),
            pltpu.VMEM((tile + 2 * HALO, cblk), F32),
            pltpu.VMEM((tile, D_MODEL), BF16),
            pltpu.VMEM((tile, D_MODEL), BF16),
            pltpu.VMEM((tile, D_MODEL), BF16),
            pltpu.VMEM((tile, HEADS * DK), F32),
            pltpu.VMEM((tile, HEADS * DV), F32),
            pltpu.VMEM((tile, D_MODEL), F32),
            pltpu.VMEM((tile, D_MODEL), F32),
            pltpu.VMEM((2 * (nchunk * NPAIR // (D_MODEL // cblk)), CHUNK, CHUNK), BF16),
        ],
        compiler_params=pltpu.CompilerParams(
            dimension_semantics=("arbitrary", "arbitrary"), vmem_limit_bytes=VMEM_LIMIT),
        name="main",
    )(x, x, x, a, sh, g, fnw, w_all, wq, w_all, w_a, w_b, w_out, conv_w, conv_b, gn_w,
      kx, v, sb, kvf, kvf_ctx, cd, cos, sin, dmask, qd)


def _qk_perm():
    perm = np.zeros(HEADS * DK, np.int32)
    for gp in range(NPAIR):
        for ln in range(LANES):
            s = ln // 32
            perm[gp * LANES + ln] = (2 * gp + (s % 2)) * DK + (s // 2) * 32 + (ln % 32)
    return perm


def _rope_tables(l):
    pos = np.arange(l)
    row = (pos // GRID_W).astype(np.float64)
    col = (pos % GRID_W).astype(np.float64)
    nf = DK // 4
    inv = ROPE_BASE ** (-np.arange(nf, dtype=np.float64) / nf)
    ang = np.concatenate([row[:, None] * inv, col[:, None] * inv], axis=-1)
    cos = np.tile(np.cos(ang), (1, 4))
    sin = np.tile(np.sin(ang), (1, 4))
    sin[:, :LANES // 2] *= -1.0
    return jnp.asarray(cos, F32), jnp.asarray(sin, F32)


def kernel(x, c, ctx, c_ctx, norm_w, ada_w, ada_b, w_in, conv_w, conv_b, decay_logit, gn_w,
           w_a, w_b, w_out, final_norm_w):
    b, l, _ = x.shape
    lc = ctx.shape[1]
    assert lc == 2 * CHUNK and l % 256 == 0 and b < 8
    kv_tile = 256
    main_tile = 256
    cblk = 256

    cc = jnp.zeros((8, D_MODEL), F32).at[:b].set(c).at[b].set(c_ctx)
    mod = _modulation(cc, ada_w[0], ada_b[0].reshape(1, -1), norm_w[0].reshape(1, -1))
    sh, a, g = mod[:, :D_MODEL], mod[:, D_MODEL:2 * D_MODEL], mod[:, 2 * D_MODEL:]
    sh_x, a_x, g_x = (t[:b].reshape(b, 1, D_MODEL) for t in (sh, a, g))
    sh_c = jnp.broadcast_to(sh[b].reshape(1, 1, D_MODEL), (b, 1, D_MODEL))
    a_c = jnp.broadcast_to(a[b].reshape(1, 1, D_MODEL), (b, 1, D_MODEL))

    dl = decay_logit[0].astype(F32)
    dlh = jnp.broadcast_to(dl[:, :, None, None], (2, HEADS, 8, LANES))
    lane_head = np.array([[2 * p + ((ln // 32) % 2) for ln in range(LANES)] for p in range(NPAIR)])
    dlp = jnp.broadcast_to(dl[:, lane_head][:, :, None, :], (2, NPAIR, 8, LANES))
    dmask, mkd, qd, cd = _decay_tables(dlh, dlp)

    w = w_in[0]
    perm = _qk_perm()
    w_all = w.astype(BF16)
    wq = w[:, _Q_COL:_K_COL][:, perm].astype(BF16)
    wk = (w[:, _K_COL:_V_COL][:, perm] * (DK ** -0.5)).astype(BF16)

    cos, sin = _rope_tables(l)
    zeros_state = jnp.zeros((b, HEADS, LANES, DV), F32)

    _, _, _, kvf_ctx, sb0 = _kv_states(ctx, a_c, sh_c, wk, w_all, cos[:lc], sin[:lc], mkd, cd,
                                       zeros_state, rotary=False, tile=lc)
    kx, v, sb, kvf, _ = _kv_states(x, a_x, sh_x, wk, w_all, cos, sin, mkd, cd, sb0,
                                   rotary=True, tile=kv_tile)

    return _main(x, a_x, sh_x, g_x, final_norm_w.reshape(1, -1), w_all, wq,
                 w_a[0].astype(BF16), w_b[0].astype(BF16), w_out[0].astype(BF16),
                 conv_w[0], conv_b[0].reshape(1, -1), gn_w[0].reshape(1, -1),
                 kx, v, sb, kvf, kvf_ctx, cd, cos, sin, dmask, qd,
                 tile=main_tile, cblk=cblk)
```

```python
import functools

import numpy as np
import jax
import jax.numpy as jnp
from jax import lax
from jax.experimental import pallas as pl
from jax.experimental.pallas import tpu as pltpu

D_MODEL = 1024
HEADS = 8
DK = 64
DV = 128
CHUNK = 128
GRID_W = 64
ROPE_BASE = 10000.0
EPS = 1e-6
NPAIR = HEADS // 2
LANES = 128
HALO = 16
_HEAD_ROWS = DK // 2
VMEM_LIMIT = 56 * 1024 * 1024

F32 = jnp.float32
BF16 = jnp.bfloat16

_Q_COL, _K_COL, _V_COL = 4096, 4608, 5120
_V_BLOCK = _V_COL // (HEADS * DV)
_A_WIDTH = 4096
_B_WIDTH = 3072
_B_BLOCK = 6144 // _B_WIDTH
_OFF_H, _OFF_BG, _OFF_CG, _OFF_ZA = 0, 1024, 2048, 3072
_OFF_ZB, _OFF_GA, _OFF_GB = 0, 1024, 2048


def _dot(a, b):
    return jnp.dot(a, b, preferred_element_type=F32)


def _log_sigmoid(x):
    return -(jnp.maximum(-x, 0.0) + jnp.log1p(jnp.exp(-jnp.abs(x))))


def _rms_scale(xv):
    return lax.rsqrt(jnp.mean(xv * xv, axis=-1, keepdims=True) + EPS)


def _mod_kernel(cc_ref, w_ref, b_ref, nw_ref, o_ref):
    j = pl.program_id(0)
    s = jax.nn.silu(cc_ref[...])
    val = jnp.dot(s, w_ref[...], preferred_element_type=F32,
                  precision=lax.Precision.HIGHEST) + b_ref[...]
    o_ref[...] = jnp.where(j == 1, nw_ref[...] * (1.0 + val), val)


def _modulation(cc, ada_w, ada_b, norm_w):
    return pl.pallas_call(
        _mod_kernel,
        grid=(3,),
        in_specs=[
            pl.BlockSpec((8, D_MODEL), lambda j: (0, 0)),
            pl.BlockSpec((D_MODEL, D_MODEL), lambda j: (0, j)),
            pl.BlockSpec((1, D_MODEL), lambda j: (0, j)),
            pl.BlockSpec((1, D_MODEL), lambda j: (0, 0)),
        ],
        out_specs=pl.BlockSpec((8, D_MODEL), lambda j: (0, j)),
        out_shape=jax.ShapeDtypeStruct((8, 3 * D_MODEL), F32),
        name="mod",
    )(cc, ada_w, ada_b, norm_w)


def _decay_kernel(dlh_ref, dlp_ref, dlr_ref, dmask_ref, kdt_ref, qd_ref, cdp_ref):
    c = float(CHUNK)
    ii = lax.broadcasted_iota(jnp.int32, (CHUNK, CHUNK), 0)
    jj = lax.broadcasted_iota(jnp.int32, (CHUNK, CHUNK), 1)
    diff = (ii - jj).astype(F32)
    row = ii.astype(F32)
    col = jj.astype(F32)
    for h in range(HEADS):
        lgf = _log_sigmoid(dlh_ref[0, h, 0:1, :])
        lgb = _log_sigmoid(dlh_ref[1, h, 0:1, :])
        dmask_ref[h] = jnp.where(
            diff > 0.0, jnp.exp(lgf * jnp.maximum(diff, 0.0)),
            jnp.where(diff < 0.0, jnp.exp(lgb * jnp.maximum(-diff, 0.0)), 2.0))
        hm = (((ii >> 5) & 1) == (h % 2)).astype(F32)
        kdt_ref[0, h] = hm * jnp.exp(lgf * (c - 1.0 - col))
        kdt_ref[1, h] = hm * jnp.exp(lgb * col)
    for p in range(NPAIR):
        lgf = _log_sigmoid(dlp_ref[0, p, 0:1, :])
        lgb = _log_sigmoid(dlp_ref[1, p, 0:1, :])
        qd_ref[0, p] = jnp.exp(lgf * (row + 1.0))
        qd_ref[1, p] = jnp.exp(lgb * (c - row))
        cdp_ref[0, p] = jnp.exp(_log_sigmoid(dlr_ref[0, p]) * c)
        cdp_ref[1, p] = jnp.exp(_log_sigmoid(dlr_ref[1, p]) * c)


def _decay_tables(dlh, dlp, dlr):
    return pl.pallas_call(
        _decay_kernel,
        out_shape=(
            jax.ShapeDtypeStruct((HEADS, CHUNK, CHUNK), F32),
            jax.ShapeDtypeStruct((2, HEADS, LANES, CHUNK), F32),
            jax.ShapeDtypeStruct((2, NPAIR, CHUNK, LANES), F32),
            jax.ShapeDtypeStruct((2, NPAIR, LANES, DV), F32),
        ),
        name="decay",
    )(dlh, dlp, dlr)


def _kv_kernel(x_ref, a_ref, sh_ref, wkt_ref, wv_ref, cos_ref, sin_ref, kdt_ref, cdp_ref, sinit_ref,
               xm_ref, kt_ref, v_ref, sb_ref, kvf_ref, sfin_ref, *, rotary, nchunk):
    i = pl.program_id(1)
    x = x_ref[0]
    xm = (x * _rms_scale(x) * a_ref[0] + sh_ref[0]).astype(BF16)
    xm_ref[0] = xm
    kt = lax.dot_general(wkt_ref[...], xm, (((1,), (1,)), ((), ())),
                         preferred_element_type=F32)
    v = _dot(xm, wv_ref[...]).astype(BF16)
    v_ref[0] = v

    @pl.when(i == 0)
    def _():
        sfin_ref[0] = sinit_ref[0]

    krs = []
    for p in range(NPAIR):
        kp = kt[p * LANES:(p + 1) * LANES, :]
        if rotary:
            swapped = jnp.concatenate([kp[LANES // 2:], kp[:LANES // 2]], axis=0)
            kp = kp * cos_ref[...] + swapped * sin_ref[...]
        krs.append(kp)
        kt_ref[0, p * LANES:(p + 1) * LANES, :] = kp.astype(BF16)

    for c in reversed(range(nchunk)):
        tok = slice(c * CHUNK, (c + 1) * CHUNK)
        for p in range(NPAIR):
            kc = krs[p][:, tok]
            inc = None
            for e in range(2):
                h = 2 * p + e
                kd = jnp.concatenate([kc * kdt_ref[0, h], kc * kdt_ref[1, h]], axis=0).astype(BF16)
                part = _dot(kd, v[tok, h * DV:(h + 1) * DV])
                inc = part if inc is None else inc + part
            sb_ref[0, c, p] = sfin_ref[0, p].astype(BF16)
            kvf_ref[0, c, p] = inc[:LANES]
            sfin_ref[0, p] = cdp_ref[1, p] * sfin_ref[0, p] + inc[LANES:]


def _kv_states(xs, a, sh, wkt, w_all, cos_t, sin_t, kdt, cdp, sinit, *, rotary, tile):
    b, l, _ = xs.shape
    nt = l // tile
    nchunk = tile // CHUNK
    nc = l // CHUNK
    rev = lambda bi, i: (bi, nt - 1 - i, 0)
    state_blk = pl.BlockSpec((1, nchunk, NPAIR, LANES, DV), lambda bi, i: (bi, nt - 1 - i, 0, 0, 0))
    return pl.pallas_call(
        functools.partial(_kv_kernel, rotary=rotary, nchunk=nchunk),
        grid=(b, nt),
        in_specs=[
            pl.BlockSpec((1, tile, D_MODEL), rev),
            pl.BlockSpec((1, 1, D_MODEL), lambda bi, i: (bi, 0, 0)),
            pl.BlockSpec((1, 1, D_MODEL), lambda bi, i: (bi, 0, 0)),
            pl.BlockSpec((HEADS * DK, D_MODEL), lambda bi, i: (0, 0)),
            pl.BlockSpec((D_MODEL, HEADS * DV), lambda bi, i: (0, _V_BLOCK)),
            pl.BlockSpec((LANES, tile), lambda bi, i: (0, nt - 1 - i)),
            pl.BlockSpec((LANES, tile), lambda bi, i: (0, nt - 1 - i)),
            pl.BlockSpec((2, HEADS, LANES, CHUNK), lambda bi, i: (0, 0, 0, 0)),
            pl.BlockSpec((2, NPAIR, LANES, DV), lambda bi, i: (0, 0, 0, 0)),
            pl.BlockSpec((1, NPAIR, LANES, DV), lambda bi, i: (bi, 0, 0, 0)),
        ],
        out_specs=(
            pl.BlockSpec((1, tile, D_MODEL), rev),
            pl.BlockSpec((1, HEADS * DK, tile), lambda bi, i: (bi, 0, nt - 1 - i)),
            pl.BlockSpec((1, tile, HEADS * DV), rev),
            state_blk,
            state_blk,
            pl.BlockSpec((1, NPAIR, LANES, DV), lambda bi, i: (bi, 0, 0, 0)),
        ),
        out_shape=(
            jax.ShapeDtypeStruct((b, l, D_MODEL), BF16),
            jax.ShapeDtypeStruct((b, HEADS * DK, l), BF16),
            jax.ShapeDtypeStruct((b, l, HEADS * DV), BF16),
            jax.ShapeDtypeStruct((b, nc, NPAIR, LANES, DV), BF16),
            jax.ShapeDtypeStruct((b, nc, NPAIR, LANES, DV), F32),
            jax.ShapeDtypeStruct((b, NPAIR, LANES, DV), F32),
        ),
        compiler_params=pltpu.CompilerParams(
            dimension_semantics=("arbitrary", "arbitrary"), vmem_limit_bytes=VMEM_LIMIT),
        name="kv_rot" if rotary else "kv_ctx",
    )(xs, a, sh, wkt, w_all, cos_t, sin_t, kdt, cdp, sinit)


def _main_kernel(x_ref, xm_ref, xmp_ref, xmn_ref, g_ref, fnw_ref,
                 wa4_ref, wq_ref, wb3_ref, wa_ref, wb_ref, wo_ref, cw_ref, cb_ref, gnw_ref,
                 kt_ref, v_ref, sb_ref, kvf_ref, kvfc_ref, cdp_ref,
                 cos_ref, sin_ref, dmask_ref, qd_ref,
                 o_ref,
                 sf_ref, xme_ref, ue_ref, ain_ref, bin_ref, m_ref, qr_ref, szb_ref, pa_ref, sgb_ref,
                 s_ref, *, tile, nt, cblk):
    i = pl.program_id(1)
    nchunk = tile // CHUNK
    nblk = D_MODEL // cblk

    @pl.when(i == 0)
    def _():
        for p in range(NPAIR):
            sf_ref[p] = cdp_ref[0, p] * kvfc_ref[0, 0, p] + kvfc_ref[0, 1, p]

    xmv = xm_ref[0]

    q = _dot(xmv, wq_ref[...])
    for p in range(NPAIR):
        qp = q[:, p * LANES:(p + 1) * LANES]
        qr_ref[:, p * LANES:(p + 1) * LANES] = (
            qp * cos_ref[...] + pltpu.roll(qp, LANES // 2, 1) * sin_ref[...])
    szb_ref[...] = jax.nn.silu(_dot(xmv, wb3_ref[:, _OFF_ZB:_OFF_ZB + HEADS * DV]))

    zero_halo = jnp.zeros((HALO, D_MODEL), BF16)
    xme_ref[0:HALO] = jnp.where(i > 0, xmp_ref[0], zero_halo)
    xme_ref[HALO:HALO + tile] = xmv
    xme_ref[HALO + tile:] = jnp.where(i < nt - 1, xmn_ref[0], zero_halo)

    def conv_block(j):
        cs = slice(j * cblk, (j + 1) * cblk)
        xe = xme_ref[...]
        hh = _dot(xe, wa4_ref[:, _OFF_H + j * cblk:_OFF_H + (j + 1) * cblk])
        cg = _dot(xe, wa4_ref[:, _OFF_CG + j * cblk:_OFF_CG + (j + 1) * cblk])
        ue_ref[...] = hh * cg
        conv = (cw_ref[0:1, cs] * ue_ref[HALO - 1:HALO - 1 + tile]
                + cw_ref[1:2, cs] * ue_ref[HALO:HALO + tile]
                + cw_ref[2:3, cs] * ue_ref[HALO + 1:HALO + 1 + tile]
                + cb_ref[:, cs])
        bg = _dot(xmv, wa4_ref[:, _OFF_BG + j * cblk:_OFF_BG + (j + 1) * cblk])
        za = _dot(xmv, wa4_ref[:, _OFF_ZA + j * cblk:_OFF_ZA + (j + 1) * cblk])
        ain_ref[:, cs] = (jax.nn.silu(za) * bg * conv).astype(BF16)

    zero_rows = jnp.zeros((_HEAD_ROWS, LANES), BF16)

    def head_rows(blocks, e):
        return [blk if (r % 2) == e else zero_rows for r, blk in enumerate(blocks)]

    def scores(c, p, slot):
        rows = slice(c * CHUNK, (c + 1) * CHUNK)
        qpb = qr_ref[rows, p * LANES:(p + 1) * LANES].astype(BF16)
        kblk = [kt_ref[0, p * LANES + r * _HEAD_ROWS:p * LANES + (r + 1) * _HEAD_ROWS, rows]
                for r in range(4)]
        for e in range(2):
            h = 2 * p + e
            s = _dot(qpb, jnp.concatenate(head_rows(kblk, e), axis=0)) * dmask_ref[h]
            s_ref[2 * slot + e] = s.astype(BF16)

    def outputs(c, p, slot):
        rows = slice(c * CHUNK, (c + 1) * CHUNK)
        qp = qr_ref[rows, p * LANES:(p + 1) * LANES]
        qfb = jnp.concatenate([(qp * qd_ref[0, p]).astype(BF16),
                               (qp * qd_ref[1, p]).astype(BF16)], axis=1)
        sfp = sf_ref[p].astype(BF16)
        sblk = ([sfp[r * _HEAD_ROWS:(r + 1) * _HEAD_ROWS] for r in range(4)]
                + [sb_ref[0, c, p, r * _HEAD_ROWS:(r + 1) * _HEAD_ROWS, :] for r in range(4)])
        for e in range(2):
            h = 2 * p + e
            hs = slice(h * DV, (h + 1) * DV)
            states = jnp.concatenate(head_rows(sblk, e), axis=0)
            o = _dot(s_ref[2 * slot + e], v_ref[0, rows, hs]) + _dot(qfb, states)
            mu = jnp.mean(o, axis=-1, keepdims=True)
            d = o - mu
            var = jnp.mean(d * d, axis=-1, keepdims=True)
            rn = d * lax.rsqrt(var + EPS) * gnw_ref[:, hs]
            bin_ref[rows, hs] = (szb_ref[rows, hs] * rn).astype(BF16)
        if p == NPAIR - 1:
            for pp in range(NPAIR):
                sf_ref[pp] = cdp_ref[0, pp] * sf_ref[pp] + kvf_ref[0, c, pp]

    units = [(c, p) for c in range(nchunk) for p in range(NPAIR)]
    per_blk = len(units) // nblk
    for j in range(nblk):
        group = units[j * per_blk:(j + 1) * per_blk]
        for slot, (c, p) in enumerate(group):
            scores(c, p, slot)
        conv_block(j)
        for slot, (c, p) in enumerate(group):
            outputs(c, p, slot)

    for j in range(nblk):
        cs = slice(j * cblk, (j + 1) * cblk)
        pa_ref[:, cs] = jax.nn.sigmoid(
            _dot(xmv, wb3_ref[:, _OFF_GA + j * cblk:_OFF_GA + (j + 1) * cblk]))
        sgb_ref[:, cs] = jax.nn.sigmoid(
            _dot(xmv, wb3_ref[:, _OFF_GB + j * cblk:_OFF_GB + (j + 1) * cblk]))
    pa_ref[...] = pa_ref[...] * _dot(ain_ref[...], wa_ref[...])

    pieces = [slice(r * CHUNK, (r + 1) * CHUNK) for r in range(nchunk)]
    for rs in pieces:
        yb = _dot(bin_ref[rs], wb_ref[...])
        m_ref[rs] = (pa_ref[rs] + sgb_ref[rs] * yb).astype(BF16)
    for rs in pieces:
        z = x_ref[0, rs] + g_ref[0] * _dot(m_ref[rs], wo_ref[...])
        o_ref[0, rs] = z * _rms_scale(z) * fnw_ref[...]


def _main(x, xm, g, fnw, w_all, wq, w_a, w_b, w_out, conv_w, conv_b, gn_w,
          kt, v, sb, kvf, kvf_ctx, cdp, cos, sin, dmask, qd, *, tile, cblk):
    b, l, _ = x.shape
    nt = l // tile
    nchunk = tile // CHUNK
    hb = tile // HALO
    nhb = l // HALO
    const = lambda *shape: pl.BlockSpec(shape, lambda bi, i: (0,) * len(shape),
                                        pipeline_mode=pl.Buffered(1))
    per_b = pl.BlockSpec((1, 1, D_MODEL), lambda bi, i: (bi, 0, 0))
    tile_rows = lambda w: pl.BlockSpec((1, tile, w), lambda bi, i: (bi, i, 0))
    chunk_state = pl.BlockSpec((1, nchunk, NPAIR, LANES, DV), lambda bi, i: (bi, i, 0, 0, 0))
    return pl.pallas_call(
        functools.partial(_main_kernel, tile=tile, nt=nt, cblk=cblk),
        grid=(b, nt),
        in_specs=[
            tile_rows(D_MODEL),
            tile_rows(D_MODEL),
            pl.BlockSpec((1, HALO, D_MODEL), lambda bi, i: (bi, jnp.maximum(i * hb - 1, 0), 0)),
            pl.BlockSpec((1, HALO, D_MODEL), lambda bi, i: (bi, jnp.minimum((i + 1) * hb, nhb - 1), 0)),
            per_b,
            const(1, D_MODEL),
            pl.BlockSpec((D_MODEL, _A_WIDTH), lambda bi, i: (0, 0), pipeline_mode=pl.Buffered(1)),
            const(D_MODEL, HEADS * DK),
            pl.BlockSpec((D_MODEL, _B_WIDTH), lambda bi, i: (0, _B_BLOCK), pipeline_mode=pl.Buffered(1)),
            const(D_MODEL, D_MODEL), const(D_MODEL, D_MODEL), const(D_MODEL, D_MODEL),
            const(3, D_MODEL), const(1, D_MODEL), const(1, D_MODEL),
            pl.BlockSpec((1, HEADS * DK, tile), lambda bi, i: (bi, 0, i)),
            tile_rows(HEADS * DV),
            chunk_state, chunk_state,
            pl.BlockSpec((1, 2, NPAIR, LANES, DV), lambda bi, i: (bi, 0, 0, 0, 0)),
            const(2, NPAIR, LANES, DV),
            pl.BlockSpec((tile, LANES), lambda bi, i: (i, 0)),
            pl.BlockSpec((tile, LANES), lambda bi, i: (i, 0)),
            const(HEADS, CHUNK, CHUNK),
            const(2, NPAIR, CHUNK, LANES),
        ],
        out_specs=tile_rows(D_MODEL),
        out_shape=jax.ShapeDtypeStruct((b, l, D_MODEL), F32),
        scratch_shapes=[
            pltpu.VMEM((NPAIR, LANES, DV), F32),
            pltpu.VMEM((tile + 2 * HALO, D_MODEL), BF16),
            pltpu.VMEM((tile + 2 * HALO, cblk), F32),
            pltpu.VMEM((tile, D_MODEL), BF16),
            pltpu.VMEM((tile, D_MODEL), BF16),
            pltpu.VMEM((tile, D_MODEL), BF16),
            pltpu.VMEM((tile, HEADS * DK), F32),
            pltpu.VMEM((tile, HEADS * DV), F32),
            pltpu.VMEM((tile, D_MODEL), F32),
            pltpu.VMEM((tile, D_MODEL), F32),
            pltpu.VMEM((2 * (nchunk * NPAIR // (D_MODEL // cblk)), CHUNK, CHUNK), BF16),
        ],
        compiler_params=pltpu.CompilerParams(
            dimension_semantics=("arbitrary", "arbitrary"), vmem_limit_bytes=VMEM_LIMIT),
        name="main",
    )(x, xm, xm, xm, g, fnw, w_all, wq, w_all, w_a, w_b, w_out, conv_w, conv_b, gn_w,
      kt, v, sb, kvf, kvf_ctx, cdp, cos, sin, dmask, qd)


def _qk_perm():
    perm = np.zeros(HEADS * DK, np.int32)
    for gp in range(NPAIR):
        for ln in range(LANES):
            s = ln // 32
            perm[gp * LANES + ln] = (2 * gp + (s % 2)) * DK + (s // 2) * 32 + (ln % 32)
    return perm


def _rope_tables(l):
    pos = np.arange(l)
    row = (pos // GRID_W).astype(np.float64)
    col = (pos % GRID_W).astype(np.float64)
    nf = DK // 4
    inv = ROPE_BASE ** (-np.arange(nf, dtype=np.float64) / nf)
    ang = np.concatenate([row[:, None] * inv, col[:, None] * inv], axis=-1)
    cos = np.tile(np.cos(ang), (1, 4))
    sin = np.tile(np.sin(ang), (1, 4))
    sin[:, :LANES // 2] *= -1.0
    as_f32 = lambda t: jnp.asarray(t, F32)
    return as_f32(cos), as_f32(sin), as_f32(cos.T), as_f32(sin.T)


def kernel(x, c, ctx, c_ctx, norm_w, ada_w, ada_b, w_in, conv_w, conv_b, decay_logit, gn_w,
           w_a, w_b, w_out, final_norm_w):
    b, l, _ = x.shape
    lc = ctx.shape[1]
    assert lc == 2 * CHUNK and l % 512 == 0 and b < 8
    kv_tile = 512
    main_tile = 512
    cblk = 256

    cc = jnp.zeros((8, D_MODEL), F32).at[:b].set(c).at[b].set(c_ctx)
    mod = _modulation(cc, ada_w[0], ada_b[0].reshape(1, -1), norm_w[0].reshape(1, -1))
    sh, a, g = mod[:, :D_MODEL], mod[:, D_MODEL:2 * D_MODEL], mod[:, 2 * D_MODEL:]
    sh_x, a_x, g_x = (t[:b].reshape(b, 1, D_MODEL) for t in (sh, a, g))
    sh_c = jnp.broadcast_to(sh[b].reshape(1, 1, D_MODEL), (b, 1, D_MODEL))
    a_c = jnp.broadcast_to(a[b].reshape(1, 1, D_MODEL), (b, 1, D_MODEL))

    dl = decay_logit[0].astype(F32)
    dlh = jnp.broadcast_to(dl[:, :, None, None], (2, HEADS, 8, LANES))
    lane_head = np.array([[2 * p + ((ln // 32) % 2) for ln in range(LANES)] for p in range(NPAIR)])
    dlp = jnp.broadcast_to(dl[:, lane_head][:, :, None, :], (2, NPAIR, 8, LANES))
    dlr = jnp.broadcast_to(dl[:, lane_head][:, :, :, None], (2, NPAIR, LANES, DV))
    dmask, kdt, qd, cdp = _decay_tables(dlh, dlp, dlr)

    w = w_in[0]
    perm = _qk_perm()
    w_all = w.astype(BF16)
    wq = w[:, _Q_COL:_K_COL][:, perm].astype(BF16)
    wkt = (w[:, _K_COL:_V_COL][:, perm] * (DK ** -0.5)).T.astype(BF16)

    cos, sin, cos_t, sin_t = _rope_tables(l)
    zeros_state = jnp.zeros((b, NPAIR, LANES, DV), F32)

    _, _, _, _, kvf_ctx, sb0 = _kv_states(ctx, a_c, sh_c, wkt, w_all, cos_t[:, :lc], sin_t[:, :lc],
                                          kdt, cdp, zeros_state, rotary=False, tile=lc)
    xm, kt, v, sb, kvf, _ = _kv_states(x, a_x, sh_x, wkt, w_all, cos_t, sin_t, kdt, cdp, sb0,
                                       rotary=True, tile=kv_tile)

    return _main(x, xm, g_x, final_norm_w.reshape(1, -1), w_all, wq,
                 w_a[0].astype(BF16), w_b[0].astype(BF16), w_out[0].astype(BF16),
                 conv_w[0], conv_b[0].reshape(1, -1), gn_w[0].reshape(1, -1),
                 kt, v, sb, kvf, kvf_ctx, cdp, cos, sin, dmask, qd,
                 tile=main_tile, cblk=cblk)
```

```python
import functools

import numpy as np
import jax
import jax.numpy as jnp
from jax import lax
from jax.experimental import pallas as pl
from jax.experimental.pallas import tpu as pltpu

D_MODEL = 1024
HEADS = 8
DK = 64
DV = 128
CHUNK = 128
GRID_W = 64
ROPE_BASE = 10000.0
EPS = 1e-6
NPAIR = HEADS // 2
LANES = 128
HALO = 16
_HEAD_ROWS = DK // 2
VMEM_LIMIT = 56 * 1024 * 1024

F32 = jnp.float32
BF16 = jnp.bfloat16

_Q_COL, _K_COL, _V_COL = 4096, 4608, 5120
_V_BLOCK = _V_COL // (HEADS * DV)
_A_WIDTH = 4096
_B_WIDTH = 3072
_B_BLOCK = 6144 // _B_WIDTH
_OFF_H, _OFF_BG, _OFF_CG, _OFF_ZA = 0, 1024, 2048, 3072
_OFF_ZB, _OFF_GA, _OFF_GB = 0, 1024, 2048


def _dot(a, b):
    return jnp.dot(a, b, preferred_element_type=F32)


def _log_sigmoid(x):
    return -(jnp.maximum(-x, 0.0) + jnp.log1p(jnp.exp(-jnp.abs(x))))


def _rms_scale(xv):
    return lax.rsqrt(jnp.mean(xv * xv, axis=-1, keepdims=True) + EPS)


def _mod_kernel(cc_ref, w_ref, b_ref, nw_ref, o_ref):
    j = pl.program_id(0)
    s = jax.nn.silu(cc_ref[...])
    val = jnp.dot(s, w_ref[...], preferred_element_type=F32,
                  precision=lax.Precision.HIGHEST) + b_ref[...]
    o_ref[...] = jnp.where(j == 1, nw_ref[...] * (1.0 + val), val)


def _modulation(cc, ada_w, ada_b, norm_w):
    return pl.pallas_call(
        _mod_kernel,
        grid=(3,),
        in_specs=[
            pl.BlockSpec((8, D_MODEL), lambda j: (0, 0)),
            pl.BlockSpec((D_MODEL, D_MODEL), lambda j: (0, j)),
            pl.BlockSpec((1, D_MODEL), lambda j: (0, j)),
            pl.BlockSpec((1, D_MODEL), lambda j: (0, 0)),
        ],
        out_specs=pl.BlockSpec((8, D_MODEL), lambda j: (0, j)),
        out_shape=jax.ShapeDtypeStruct((8, 3 * D_MODEL), F32),
        name="mod",
    )(cc, ada_w, ada_b, norm_w)


def _decay_kernel(dlh_ref, dlp_ref, dlr_ref, dmask_ref, kdt_ref, qd_ref, cdp_ref):
    c = float(CHUNK)
    ii = lax.broadcasted_iota(jnp.int32, (CHUNK, CHUNK), 0)
    jj = lax.broadcasted_iota(jnp.int32, (CHUNK, CHUNK), 1)
    diff = (ii - jj).astype(F32)
    row = ii.astype(F32)
    col = jj.astype(F32)
    for h in range(HEADS):
        lgf = _log_sigmoid(dlh_ref[0, h, 0:1, :])
        lgb = _log_sigmoid(dlh_ref[1, h, 0:1, :])
        dmask_ref[h] = jnp.where(
            diff > 0.0, jnp.exp(lgf * jnp.maximum(diff, 0.0)),
            jnp.where(diff < 0.0, jnp.exp(lgb * jnp.maximum(-diff, 0.0)), 2.0))
    for p in range(NPAIR):
        lgf = _log_sigmoid(dlp_ref[0, p, 0:1, :])
        lgb = _log_sigmoid(dlp_ref[1, p, 0:1, :])
        qd_ref[0, p] = jnp.exp(lgf * (row + 1.0))
        qd_ref[1, p] = jnp.exp(lgb * (c - row))
        lrf = _log_sigmoid(dlr_ref[0, p])
        lrb = _log_sigmoid(dlr_ref[1, p])
        kdt_ref[0, p] = jnp.exp(lrf * (c - 1.0 - col))
        kdt_ref[1, p] = jnp.exp(lrb * col)
        cdp_ref[0, p] = jnp.exp(lrf * c)
        cdp_ref[1, p] = jnp.exp(lrb * c)


def _decay_tables(dlh, dlp, dlr):
    return pl.pallas_call(
        _decay_kernel,
        out_shape=(
            jax.ShapeDtypeStruct((HEADS, CHUNK, CHUNK), F32),
            jax.ShapeDtypeStruct((2, NPAIR, LANES, CHUNK), F32),
            jax.ShapeDtypeStruct((2, NPAIR, CHUNK, LANES), F32),
            jax.ShapeDtypeStruct((2, NPAIR, LANES, DV), F32),
        ),
        name="decay",
    )(dlh, dlp, dlr)


def _kv_kernel(x_ref, a_ref, sh_ref, wkt_ref, wv_ref, cos_ref, sin_ref, kdt_ref, cdp_ref, sinit_ref,
               xm_ref, kt_ref, v_ref, sb_ref, kvf_ref, sfin_ref, *, rotary, nchunk):
    i = pl.program_id(1)

    @pl.when(i == 0)
    def _():
        sfin_ref[0] = sinit_ref[0]

    row = lax.broadcasted_iota(jnp.int32, (2 * LANES, DV), 0)
    is_a = ((row >> 5) & 1) == 0

    span = 2 * CHUNK

    def project(g):
        tok = slice(g * span, (g + 1) * span)
        x = x_ref[0, tok]
        xm = (x * _rms_scale(x) * a_ref[0] + sh_ref[0]).astype(BF16)
        xm_ref[0, tok] = xm
        kt = lax.dot_general(wkt_ref[...], xm, (((1,), (1,)), ((), ())),
                             preferred_element_type=F32)
        v = _dot(xm, wv_ref[...]).astype(BF16)
        v_ref[0, tok] = v
        krs = []
        for p in range(NPAIR):
            kp = kt[p * LANES:(p + 1) * LANES]
            if rotary:
                swapped = jnp.concatenate([kp[LANES // 2:], kp[:LANES // 2]], axis=0)
                kp = kp * cos_ref[:, tok] + swapped * sin_ref[:, tok]
            krs.append(kp)
            kt_ref[0, p * LANES:(p + 1) * LANES, tok] = kp.astype(BF16)
        return krs, v

    def advance(g, krs, v):
        for c in (1, 0):
            tok = slice(c * CHUNK, (c + 1) * CHUNK)
            for p in range(NPAIR):
                kc = krs[p][:, tok]
                kd = jnp.concatenate([kc * kdt_ref[0, p], kc * kdt_ref[1, p]], axis=0).astype(BF16)
                both = _dot(kd, v[tok, 2 * p * DV:2 * (p + 1) * DV])
                inc = jnp.where(is_a, both[:, :DV], both[:, DV:])
                sb_ref[0, 2 * g + c, p] = sfin_ref[0, p].astype(BF16)
                kvf_ref[0, 2 * g + c, p] = inc[:LANES]
                sfin_ref[0, p] = cdp_ref[1, p] * sfin_ref[0, p] + inc[LANES:]

    pending = None
    for g in reversed(range(nchunk // 2)):
        done = project(g)
        if pending is not None:
            advance(*pending)
        pending = (g,) + done
    advance(*pending)


def _kv_states(xs, a, sh, wkt, w_all, cos_t, sin_t, kdt, cdp, sinit, *, rotary, tile):
    b, l, _ = xs.shape
    nt = l // tile
    nchunk = tile // CHUNK
    nc = l // CHUNK
    rev = lambda bi, i: (bi, nt - 1 - i, 0)
    state_blk = pl.BlockSpec((1, nchunk, NPAIR, LANES, DV), lambda bi, i: (bi, nt - 1 - i, 0, 0, 0))
    return pl.pallas_call(
        functools.partial(_kv_kernel, rotary=rotary, nchunk=nchunk),
        grid=(b, nt),
        in_specs=[
            pl.BlockSpec((1, tile, D_MODEL), rev),
            pl.BlockSpec((1, 1, D_MODEL), lambda bi, i: (bi, 0, 0)),
            pl.BlockSpec((1, 1, D_MODEL), lambda bi, i: (bi, 0, 0)),
            pl.BlockSpec((HEADS * DK, D_MODEL), lambda bi, i: (0, 0)),
            pl.BlockSpec((D_MODEL, HEADS * DV), lambda bi, i: (0, _V_BLOCK)),
            pl.BlockSpec((LANES, tile), lambda bi, i: (0, nt - 1 - i)),
            pl.BlockSpec((LANES, tile), lambda bi, i: (0, nt - 1 - i)),
            pl.BlockSpec((2, NPAIR, LANES, CHUNK), lambda bi, i: (0, 0, 0, 0)),
            pl.BlockSpec((2, NPAIR, LANES, DV), lambda bi, i: (0, 0, 0, 0)),
            pl.BlockSpec((1, NPAIR, LANES, DV), lambda bi, i: (bi, 0, 0, 0)),
        ],
        out_specs=(
            pl.BlockSpec((1, tile, D_MODEL), rev),
            pl.BlockSpec((1, HEADS * DK, tile), lambda bi, i: (bi, 0, nt - 1 - i)),
            pl.BlockSpec((1, tile, HEADS * DV), rev),
            state_blk,
            state_blk,
            pl.BlockSpec((1, NPAIR, LANES, DV), lambda bi, i: (bi, 0, 0, 0)),
        ),
        out_shape=(
            jax.ShapeDtypeStruct((b, l, D_MODEL), BF16),
            jax.ShapeDtypeStruct((b, HEADS * DK, l), BF16),
            jax.ShapeDtypeStruct((b, l, HEADS * DV), BF16),
            jax.ShapeDtypeStruct((b, nc, NPAIR, LANES, DV), BF16),
            jax.ShapeDtypeStruct((b, nc, NPAIR, LANES, DV), F32),
            jax.ShapeDtypeStruct((b, NPAIR, LANES, DV), F32),
        ),
        compiler_params=pltpu.CompilerParams(
            dimension_semantics=("arbitrary", "arbitrary"), vmem_limit_bytes=VMEM_LIMIT),
        name="kv_rot" if rotary else "kv_ctx",
    )(xs, a, sh, wkt, w_all, cos_t, sin_t, kdt, cdp, sinit)


def _main_kernel(x_ref, xm_ref, xmp_ref, xmn_ref, g_ref, fnw_ref,
                 wa4_ref, wq_ref, wb3_ref, wa_ref, wb_ref, wo_ref, cw_ref, cb_ref, gnw_ref,
                 kt_ref, v_ref, sb_ref, kvf_ref, kvfc_ref, cdp_ref,
                 cos_ref, sin_ref, dmask_ref, qd_ref,
                 o_ref,
                 sf_ref, xme_ref, ue_ref, ain_ref, bin_ref, m_ref, qr_ref, szb_ref, pa_ref, sgb_ref,
                 s_ref, *, tile, nt, cblk):
    i = pl.program_id(1)
    nchunk = tile // CHUNK
    nblk = D_MODEL // cblk

    @pl.when(i == 0)
    def _():
        for p in range(NPAIR):
            sf_ref[p] = cdp_ref[0, p] * kvfc_ref[0, 0, p] + kvfc_ref[0, 1, p]

    xmv = xm_ref[0]

    q = _dot(xmv, wq_ref[...])
    for p in range(NPAIR):
        qp = q[:, p * LANES:(p + 1) * LANES]
        qr_ref[:, p * LANES:(p + 1) * LANES] = (
            qp * cos_ref[...] + pltpu.roll(qp, LANES // 2, 1) * sin_ref[...])
    szb_ref[...] = jax.nn.silu(_dot(xmv, wb3_ref[:, _OFF_ZB:_OFF_ZB + HEADS * DV]))

    zero_halo = jnp.zeros((HALO, D_MODEL), BF16)
    xme_ref[0:HALO] = jnp.where(i > 0, xmp_ref[0], zero_halo)
    xme_ref[HALO:HALO + tile] = xmv
    xme_ref[HALO + tile:] = jnp.where(i < nt - 1, xmn_ref[0], zero_halo)

    def conv_block(j):
        cs = slice(j * cblk, (j + 1) * cblk)
        xe = xme_ref[...]
        hh = _dot(xe, wa4_ref[:, _OFF_H + j * cblk:_OFF_H + (j + 1) * cblk])
        cg = _dot(xe, wa4_ref[:, _OFF_CG + j * cblk:_OFF_CG + (j + 1) * cblk])
        ue_ref[...] = hh * cg
        conv = (cw_ref[0:1, cs] * ue_ref[HALO - 1:HALO - 1 + tile]
                + cw_ref[1:2, cs] * ue_ref[HALO:HALO + tile]
                + cw_ref[2:3, cs] * ue_ref[HALO + 1:HALO + 1 + tile]
                + cb_ref[:, cs])
        bg = _dot(xmv, wa4_ref[:, _OFF_BG + j * cblk:_OFF_BG + (j + 1) * cblk])
        za = _dot(xmv, wa4_ref[:, _OFF_ZA + j * cblk:_OFF_ZA + (j + 1) * cblk])
        ain_ref[:, cs] = (jax.nn.silu(za) * bg * conv).astype(BF16)

    zero_rows = jnp.zeros((_HEAD_ROWS, LANES), BF16)

    def head_rows(blocks, e):
        return [blk if (r % 2) == e else zero_rows for r, blk in enumerate(blocks)]

    def scores(c, p, slot):
        rows = slice(c * CHUNK, (c + 1) * CHUNK)
        qpb = qr_ref[rows, p * LANES:(p + 1) * LANES].astype(BF16)
        kblk = [kt_ref[0, p * LANES + r * _HEAD_ROWS:p * LANES + (r + 1) * _HEAD_ROWS, rows]
                for r in range(4)]
        for e in range(2):
            h = 2 * p + e
            s = _dot(qpb, jnp.concatenate(head_rows(kblk, e), axis=0)) * dmask_ref[h]
            s_ref[2 * slot + e] = s.astype(BF16)

    def outputs(c, p, slot):
        rows = slice(c * CHUNK, (c + 1) * CHUNK)
        qp = qr_ref[rows, p * LANES:(p + 1) * LANES]
        qfb = jnp.concatenate([(qp * qd_ref[0, p]).astype(BF16),
                               (qp * qd_ref[1, p]).astype(BF16)], axis=1)
        sfp = sf_ref[p].astype(BF16)
        sblk = ([sfp[r * _HEAD_ROWS:(r + 1) * _HEAD_ROWS] for r in range(4)]
                + [sb_ref[0, c, p, r * _HEAD_ROWS:(r + 1) * _HEAD_ROWS, :] for r in range(4)])
        for e in range(2):
            h = 2 * p + e
            hs = slice(h * DV, (h + 1) * DV)
            states = jnp.concatenate(head_rows(sblk, e), axis=0)
            o = _dot(s_ref[2 * slot + e], v_ref[0, rows, hs]) + _dot(qfb, states)
            mu = jnp.mean(o, axis=-1, keepdims=True)
            d = o - mu
            var = jnp.mean(d * d, axis=-1, keepdims=True)
            rn = d * lax.rsqrt(var + EPS) * gnw_ref[:, hs]
            bin_ref[rows, hs] = (szb_ref[rows, hs] * rn).astype(BF16)
        if p == NPAIR - 1:
            for pp in range(NPAIR):
                sf_ref[pp] = cdp_ref[0, pp] * sf_ref[pp] + kvf_ref[0, c, pp]

    units = [(c, p) for c in range(nchunk) for p in range(NPAIR)]
    per_blk = len(units) // nblk
    for j in range(nblk):
        group = units[j * per_blk:(j + 1) * per_blk]
        for slot, (c, p) in enumerate(group):
            scores(c, p, slot)
        conv_block(j)
        for slot, (c, p) in enumerate(group):
            outputs(c, p, slot)

    for j in range(nblk):
        cs = slice(j * cblk, (j + 1) * cblk)
        pa_ref[:, cs] = jax.nn.sigmoid(
            _dot(xmv, wb3_ref[:, _OFF_GA + j * cblk:_OFF_GA + (j + 1) * cblk]))
        sgb_ref[:, cs] = jax.nn.sigmoid(
            _dot(xmv, wb3_ref[:, _OFF_GB + j * cblk:_OFF_GB + (j + 1) * cblk]))
    pa_ref[...] = pa_ref[...] * _dot(ain_ref[...], wa_ref[...])

    pieces = [slice(r * CHUNK, (r + 1) * CHUNK) for r in range(nchunk)]
    for rs in pieces:
        yb = _dot(bin_ref[rs], wb_ref[...])
        m_ref[rs] = (pa_ref[rs] + sgb_ref[rs] * yb).astype(BF16)
    for rs in pieces:
        z = x_ref[0, rs] + g_ref[0] * _dot(m_ref[rs], wo_ref[...])
        o_ref[0, rs] = z * _rms_scale(z) * fnw_ref[...]


def _main(x, xm, g, fnw, w_all, wq, w_a, w_b, w_out, conv_w, conv_b, gn_w,
          kt, v, sb, kvf, kvf_ctx, cdp, cos, sin, dmask, qd, *, tile, cblk):
    b, l, _ = x.shape
    nt = l // tile
    nchunk = tile // CHUNK
    hb = tile // HALO
    nhb = l // HALO
    const = lambda *shape: pl.BlockSpec(shape, lambda bi, i: (0,) * len(shape),
                                        pipeline_mode=pl.Buffered(1))
    per_b = pl.BlockSpec((1, 1, D_MODEL), lambda bi, i: (bi, 0, 0))
    tile_rows = lambda w: pl.BlockSpec((1, tile, w), lambda bi, i: (bi, i, 0))
    chunk_state = pl.BlockSpec((1, nchunk, NPAIR, LANES, DV), lambda bi, i: (bi, i, 0, 0, 0))
    return pl.pallas_call(
        functools.partial(_main_kernel, tile=tile, nt=nt, cblk=cblk),
        grid=(b, nt),
        in_specs=[
            tile_rows(D_MODEL),
            tile_rows(D_MODEL),
            pl.BlockSpec((1, HALO, D_MODEL), lambda bi, i: (bi, jnp.maximum(i * hb - 1, 0), 0)),
            pl.BlockSpec((1, HALO, D_MODEL), lambda bi, i: (bi, jnp.minimum((i + 1) * hb, nhb - 1), 0)),
            per_b,
            const(1, D_MODEL),
            pl.BlockSpec((D_MODEL, _A_WIDTH), lambda bi, i: (0, 0), pipeline_mode=pl.Buffered(1)),
            const(D_MODEL, HEADS * DK),
            pl.BlockSpec((D_MODEL, _B_WIDTH), lambda bi, i: (0, _B_BLOCK), pipeline_mode=pl.Buffered(1)),
            const(D_MODEL, D_MODEL), const(D_MODEL, D_MODEL), const(D_MODEL, D_MODEL),
            const(3, D_MODEL), const(1, D_MODEL), const(1, D_MODEL),
            pl.BlockSpec((1, HEADS * DK, tile), lambda bi, i: (bi, 0, i)),
            tile_rows(HEADS * DV),
            chunk_state, chunk_state,
            pl.BlockSpec((1, 2, NPAIR, LANES, DV), lambda bi, i: (bi, 0, 0, 0, 0)),
            const(2, NPAIR, LANES, DV),
            pl.BlockSpec((tile, LANES), lambda bi, i: (i, 0)),
            pl.BlockSpec((tile, LANES), lambda bi, i: (i, 0)),
            const(HEADS, CHUNK, CHUNK),
            const(2, NPAIR, CHUNK, LANES),
        ],
        out_specs=tile_rows(D_MODEL),
        out_shape=jax.ShapeDtypeStruct((b, l, D_MODEL), F32),
        scratch_shapes=[
            pltpu.VMEM((NPAIR, LANES, DV), F32),
            pltpu.VMEM((tile + 2 * HALO, D_MODEL), BF16),
            pltpu.VMEM((tile + 2 * HALO, cblk), F32),
            pltpu.VMEM((tile, D_MODEL), BF16),
            pltpu.VMEM((tile, D_MODEL), BF16),
            pltpu.VMEM((tile, D_MODEL), BF16),
            pltpu.VMEM((tile, HEADS * DK), F32),
            pltpu.VMEM((tile, HEADS * DV), F32),
            pltpu.VMEM((tile, D_MODEL), F32),
            pltpu.VMEM((tile, D_MODEL), F32),
            pltpu.VMEM((2 * (nchunk * NPAIR // (D_MODEL // cblk)), CHUNK, CHUNK), BF16),
        ],
        compiler_params=pltpu.CompilerParams(
            dimension_semantics=("arbitrary", "arbitrary"), vmem_limit_bytes=VMEM_LIMIT),
        name="main",
    )(x, xm, xm, xm, g, fnw, w_all, wq, w_all, w_a, w_b, w_out, conv_w, conv_b, gn_w,
      kt, v, sb, kvf, kvf_ctx, cdp, cos, sin, dmask, qd)


def _qk_perm():
    perm = np.zeros(HEADS * DK, np.int32)
    for gp in range(NPAIR):
        for ln in range(LANES):
            s = ln // 32
            perm[gp * LANES + ln] = (2 * gp + (s % 2)) * DK + (s // 2) * 32 + (ln % 32)
    return perm


def _rope_tables(l):
    pos = np.arange(l)
    row = (pos // GRID_W).astype(np.float64)
    col = (pos % GRID_W).astype(np.float64)
    nf = DK // 4
    inv = ROPE_BASE ** (-np.arange(nf, dtype=np.float64) / nf)
    ang = np.concatenate([row[:, None] * inv, col[:, None] * inv], axis=-1)
    cos = np.tile(np.cos(ang), (1, 4))
    sin = np.tile(np.sin(ang), (1, 4))
    sin[:, :LANES // 2] *= -1.0
    as_f32 = lambda t: jnp.asarray(t, F32)
    return as_f32(cos), as_f32(sin), as_f32(cos.T), as_f32(sin.T)


def kernel(x, c, ctx, c_ctx, norm_w, ada_w, ada_b, w_in, conv_w, conv_b, decay_logit, gn_w,
           w_a, w_b, w_out, final_norm_w):
    b, l, _ = x.shape
    lc = ctx.shape[1]
    assert lc == 2 * CHUNK and l % 1024 == 0 and b < 8
    kv_tile = 1024
    main_tile = 512
    cblk = 256

    cc = jnp.zeros((8, D_MODEL), F32).at[:b].set(c).at[b].set(c_ctx)
    mod = _modulation(cc, ada_w[0], ada_b[0].reshape(1, -1), norm_w[0].reshape(1, -1))
    sh, a, g = mod[:, :D_MODEL], mod[:, D_MODEL:2 * D_MODEL], mod[:, 2 * D_MODEL:]
    sh_x, a_x, g_x = (t[:b].reshape(b, 1, D_MODEL) for t in (sh, a, g))
    sh_c = jnp.broadcast_to(sh[b].reshape(1, 1, D_MODEL), (b, 1, D_MODEL))
    a_c = jnp.broadcast_to(a[b].reshape(1, 1, D_MODEL), (b, 1, D_MODEL))

    dl = decay_logit[0].astype(F32)
    dlh = jnp.broadcast_to(dl[:, :, None, None], (2, HEADS, 8, LANES))
    lane_head = np.array([[2 * p + ((ln // 32) % 2) for ln in range(LANES)] for p in range(NPAIR)])
    dlp = jnp.broadcast_to(dl[:, lane_head][:, :, None, :], (2, NPAIR, 8, LANES))
    dlr = jnp.broadcast_to(dl[:, lane_head][:, :, :, None], (2, NPAIR, LANES, DV))
    dmask, kdt, qd, cdp = _decay_tables(dlh, dlp, dlr)

    w = w_in[0]
    perm = _qk_perm()
    w_all = w.astype(BF16)
    wq = w[:, _Q_COL:_K_COL][:, perm].astype(BF16)
    wkt = (w[:, _K_COL:_V_COL][:, perm] * (DK ** -0.5)).T.astype(BF16)

    cos, sin, cos_t, sin_t = _rope_tables(l)
    zeros_state = jnp.zeros((b, NPAIR, LANES, DV), F32)

    _, _, _, _, kvf_ctx, sb0 = _kv_states(ctx, a_c, sh_c, wkt, w_all, cos_t[:, :lc], sin_t[:, :lc],
                                          kdt, cdp, zeros_state, rotary=False, tile=lc)
    xm, kt, v, sb, kvf, _ = _kv_states(x, a_x, sh_x, wkt, w_all, cos_t, sin_t, kdt, cdp, sb0,
                                       rotary=True, tile=kv_tile)

    return _main(x, xm, g_x, final_norm_w.reshape(1, -1), w_all, wq,
                 w_a[0].astype(BF16), w_b[0].astype(BF16), w_out[0].astype(BF16),
                 conv_w[0], conv_b[0].reshape(1, -1), gn_w[0].reshape(1, -1),
                 kt, v, sb, kvf, kvf_ctx, cdp, cos, sin, dmask, qd,
                 tile=main_tile, cblk=cblk)
```

```python
import functools

import numpy as np
import jax
import jax.numpy as jnp
from jax import lax
from jax.experimental import pallas as pl
from jax.experimental.pallas import tpu as pltpu

D_MODEL = 1024
HEADS = 8
DK = 64
DV = 128
CHUNK = 128
GRID_W = 64
ROPE_BASE = 10000.0
EPS = 1e-6
NPAIR = HEADS // 2
LANES = 128
HALF = DK // 2
HALO = 16
K_SCALE = DK ** -0.5
VMEM_LIMIT = 56 * 1024 * 1024

F32 = jnp.float32
BF16 = jnp.bfloat16

_Q_COL, _K_COL, _V_COL = 4096, 4608, 5120
_Q_BLOCK = _Q_COL // (HEADS * DK)
_V_BLOCK = _V_COL // (HEADS * DV)
_A_WIDTH = 4096
_B_WIDTH = 3072
_B_BLOCK = 6144 // _B_WIDTH
_OFF_H, _OFF_BG, _OFF_CG, _OFF_ZA = 0, 1024, 2048, 3072
_OFF_ZB, _OFF_GA, _OFF_GB = 0, 1024, 2048
_MOD_SHIFT, _MOD_SCALE, _MOD_GATE = 0, D_MODEL, 2 * D_MODEL


def _dot(a, b):
    return jnp.dot(a, b, preferred_element_type=F32)


def _log_sigmoid(x):
    return -(jnp.maximum(-x, 0.0) + jnp.log1p(jnp.exp(-jnp.abs(x))))


def _rms_scale(xv):
    return lax.rsqrt(jnp.mean(xv * xv, axis=-1, keepdims=True) + EPS)


def _mod_kernel(c_ref, cctx_ref, w_ref, b_ref, nw_ref, o_ref, cc_ref):
    j = pl.program_id(0)
    nb = c_ref.shape[0]
    cc_ref[...] = jnp.zeros_like(cc_ref)
    cc_ref[0:nb] = c_ref[...]
    cc_ref[nb:nb + 1] = cctx_ref[...]
    s = jax.nn.silu(cc_ref[...])
    val = jnp.dot(s, w_ref[...], preferred_element_type=F32,
                  precision=lax.Precision.HIGHEST) + b_ref[...]
    o_ref[...] = jnp.where(j == 1, nw_ref[...] * (1.0 + val), val)


def _modulation(c, c_ctx, ada_w, ada_b, norm_w):
    nb = c.shape[0]
    return pl.pallas_call(
        _mod_kernel,
        grid=(3,),
        in_specs=[
            pl.BlockSpec((nb, D_MODEL), lambda j: (0, 0)),
            pl.BlockSpec((1, D_MODEL), lambda j: (0, 0)),
            pl.BlockSpec((D_MODEL, D_MODEL), lambda j: (0, j)),
            pl.BlockSpec((1, D_MODEL), lambda j: (0, j)),
            pl.BlockSpec((1, D_MODEL), lambda j: (0, 0)),
        ],
        out_specs=pl.BlockSpec((8, D_MODEL), lambda j: (0, j)),
        out_shape=jax.ShapeDtypeStruct((8, 3 * D_MODEL), F32),
        scratch_shapes=[pltpu.VMEM((8, D_MODEL), F32)],
        name="mod",
    )(c, c_ctx, ada_w, ada_b, norm_w)


def _decay_kernel(dl_ref, dmask_ref, kdt_ref, qd_ref, cdp_ref):
    c = float(CHUNK)
    ii = lax.broadcasted_iota(jnp.int32, (CHUNK, CHUNK), 0)
    jj = lax.broadcasted_iota(jnp.int32, (CHUNK, CHUNK), 1)
    diff = (ii - jj).astype(F32)
    row = ii.astype(F32)
    col = jj.astype(F32)
    lg = [[_log_sigmoid(jnp.full((1, LANES), dl_ref[d, h], F32)) for h in range(HEADS)]
          for d in range(2)]
    for h in range(HEADS):
        dmask_ref[h] = jnp.where(
            diff > 0.0, jnp.exp(lg[0][h] * jnp.maximum(diff, 0.0)),
            jnp.where(diff < 0.0, jnp.exp(lg[1][h] * jnp.maximum(-diff, 0.0)), 2.0))
    lane_b = jj[0:1, :] >= DK
    row_b = ii >= DK
    for p in range(NPAIR):
        lgf = jnp.where(lane_b, lg[0][2 * p + 1], lg[0][2 * p])
        lgb = jnp.where(lane_b, lg[1][2 * p + 1], lg[1][2 * p])
        qd_ref[0, p] = jnp.exp(lgf * (row + 1.0))
        qd_ref[1, p] = jnp.exp(lgb * (c - row))
        lrf = jnp.where(row_b, lg[0][2 * p + 1], lg[0][2 * p])
        lrb = jnp.where(row_b, lg[1][2 * p + 1], lg[1][2 * p])
        kdt_ref[0, p] = jnp.exp(lrf * (c - 1.0 - col))
        kdt_ref[1, p] = jnp.exp(lrb * col)
        cdp_ref[0, p] = jnp.exp(lrf * c)
        cdp_ref[1, p] = jnp.exp(lrb * c)


def _decay_tables(dl):
    return pl.pallas_call(
        _decay_kernel,
        in_specs=[pl.BlockSpec(memory_space=pltpu.SMEM)],
        out_shape=(
            jax.ShapeDtypeStruct((HEADS, CHUNK, CHUNK), F32),
            jax.ShapeDtypeStruct((2, NPAIR, LANES, CHUNK), F32),
            jax.ShapeDtypeStruct((2, NPAIR, CHUNK, LANES), F32),
            jax.ShapeDtypeStruct((2, NPAIR, LANES, DV), F32),
        ),
        name="decay",
    )(dl)


def _kv_kernel(*refs, rotary, zero_init, mod_row, nchunk):
    refs = list(refs)
    x_ref, mod_ref, wkt_ref, wv_ref = refs[:4]
    del refs[:4]
    cos_ref, sin_ref = (refs.pop(0), refs.pop(0)) if rotary else (None, None)
    kdt_ref, cdp_ref = refs.pop(0), refs.pop(0)
    sinit_ref = None if zero_init else refs.pop(0)
    xm_ref, kt_ref, v_ref, sb_ref, kvf_ref, sfin_ref = refs
    i = pl.program_id(1)
    mrow = pl.program_id(0) if mod_row is None else mod_row
    shift = mod_ref[pl.ds(mrow, 1), _MOD_SHIFT:_MOD_SHIFT + D_MODEL]
    scale = mod_ref[pl.ds(mrow, 1), _MOD_SCALE:_MOD_SCALE + D_MODEL]

    @pl.when(i == 0)
    def _():
        sfin_ref[0] = jnp.zeros_like(sfin_ref[0]) if zero_init else sinit_ref[0]

    row = lax.broadcasted_iota(jnp.int32, (2 * LANES, DV), 0)
    is_a = ((row >> 6) & 1) == 0

    span = 2 * CHUNK

    def project(g):
        tok = slice(g * span, (g + 1) * span)
        x = x_ref[0, tok]
        xm = (x * _rms_scale(x) * scale + shift).astype(BF16)
        xm_ref[0, tok] = xm
        kt = lax.dot_general(wkt_ref[...], xm, (((1,), (1,)), ((), ())),
                             preferred_element_type=F32)
        v = _dot(xm, wv_ref[...]).astype(BF16)
        v_ref[0, tok] = v
        krs = []
        for p in range(NPAIR):
            kp = kt[p * LANES:(p + 1) * LANES]
            if rotary:
                swapped = jnp.concatenate(
                    [kp[HALF:DK], kp[:HALF], kp[DK + HALF:], kp[DK:DK + HALF]], axis=0)
                kp = kp * cos_ref[:, tok] + swapped * sin_ref[:, tok]
            else:
                kp = kp * K_SCALE
            krs.append(kp)
            kt_ref[0, p * LANES:(p + 1) * LANES, tok] = kp.astype(BF16)
        return krs, v

    def advance(g, krs, v):
        for c in (1, 0):
            tok = slice(c * CHUNK, (c + 1) * CHUNK)
            for p in range(NPAIR):
                kc = krs[p][:, tok]
                kd = jnp.concatenate([kc * kdt_ref[0, p], kc * kdt_ref[1, p]], axis=0).astype(BF16)
                both = _dot(kd, v[tok, 2 * p * DV:2 * (p + 1) * DV])
                inc = jnp.where(is_a, both[:, :DV], both[:, DV:])
                sb_ref[0, 2 * g + c, p] = sfin_ref[0, p].astype(BF16)
                kvf_ref[0, 2 * g + c, p] = inc[:LANES]
                sfin_ref[0, p] = cdp_ref[1, p] * sfin_ref[0, p] + inc[LANES:]

    pending = None
    for g in reversed(range(nchunk // 2)):
        done = project(g)
        if pending is not None:
            advance(*pending)
        pending = (g,) + done
    advance(*pending)


def _kv_states(xs, mod, wkt, w_all, rope_t, kdt, cdp, sinit, *, mod_row, tile):
    b, l, _ = xs.shape
    nt = l // tile
    nchunk = tile // CHUNK
    nc = l // CHUNK
    rotary = rope_t is not None
    zero_init = sinit is None
    rev = lambda bi, i: (bi, nt - 1 - i, 0)
    const = lambda *shape: pl.BlockSpec(shape, lambda bi, i: (0,) * len(shape))
    state_blk = pl.BlockSpec((1, nchunk, NPAIR, LANES, DV), lambda bi, i: (bi, nt - 1 - i, 0, 0, 0))
    pair_state = pl.BlockSpec((1, NPAIR, LANES, DV), lambda bi, i: (bi, 0, 0, 0))
    in_specs = [
        pl.BlockSpec((1, tile, D_MODEL), rev),
        const(8, 3 * D_MODEL),
        const(HEADS * DK, D_MODEL),
        pl.BlockSpec((D_MODEL, HEADS * DV), lambda bi, i: (0, _V_BLOCK)),
    ]
    args = [xs, mod, wkt, w_all]
    if rotary:
        in_specs += [pl.BlockSpec((LANES, tile), lambda bi, i: (0, nt - 1 - i))] * 2
        args += list(rope_t)
    in_specs += [const(2, NPAIR, LANES, CHUNK), const(2, NPAIR, LANES, DV)]
    args += [kdt, cdp]
    if not zero_init:
        in_specs.append(pair_state)
        args.append(sinit)
    return pl.pallas_call(
        functools.partial(_kv_kernel, rotary=rotary, zero_init=zero_init, mod_row=mod_row,
                          nchunk=nchunk),
        grid=(b, nt),
        in_specs=in_specs,
        out_specs=(
            pl.BlockSpec((1, tile, D_MODEL), rev),
            pl.BlockSpec((1, HEADS * DK, tile), lambda bi, i: (bi, 0, nt - 1 - i)),
            pl.BlockSpec((1, tile, HEADS * DV), rev),
            state_blk,
            state_blk,
            pair_state,
        ),
        out_shape=(
            jax.ShapeDtypeStruct((b, l, D_MODEL), BF16),
            jax.ShapeDtypeStruct((b, HEADS * DK, l), BF16),
            jax.ShapeDtypeStruct((b, l, HEADS * DV), BF16),
            jax.ShapeDtypeStruct((b, nc, NPAIR, LANES, DV), BF16),
            jax.ShapeDtypeStruct((b, nc, NPAIR, LANES, DV), F32),
            jax.ShapeDtypeStruct((b, NPAIR, LANES, DV), F32),
        ),
        compiler_params=pltpu.CompilerParams(
            dimension_semantics=("arbitrary", "arbitrary"), vmem_limit_bytes=VMEM_LIMIT),
        name="kv_rot" if rotary else "kv_ctx",
    )(*args)


def _main_kernel(x_ref, xm_ref, xmp_ref, xmn_ref, mod_ref, fnw_ref,
                 wa4_ref, wq_ref, wb3_ref, wa_ref, wb_ref, wo_ref, cw_ref, cb_ref, gnw_ref,
                 kt_ref, v_ref, sb_ref, kvf_ref, kvfc_ref, cdp_ref,
                 cos_ref, sin_ref, dmask_ref, qd_ref,
                 o_ref,
                 sf_ref, xme_ref, ue_ref, ain_ref, bin_ref, m_ref, qr_ref, szb_ref, pa_ref, sgb_ref,
                 s_ref, *, tile, nt, cblk):
    i = pl.program_id(1)
    nchunk = tile // CHUNK
    nblk = D_MODEL // cblk

    @pl.when(i == 0)
    def _():
        for p in range(NPAIR):
            sf_ref[p] = cdp_ref[0, p] * kvfc_ref[0, 0, p] + kvfc_ref[0, 1, p]

    xmv = xm_ref[0]

    q = _dot(xmv, wq_ref[...])
    first_half = (lax.broadcasted_iota(jnp.int32, (1, LANES), 1) & HALF) == 0
    for p in range(NPAIR):
        qp = q[:, p * LANES:(p + 1) * LANES]
        partner = jnp.where(first_half, pltpu.roll(qp, LANES - HALF, 1), pltpu.roll(qp, HALF, 1))
        qr_ref[:, p * LANES:(p + 1) * LANES] = qp * cos_ref[...] + partner * sin_ref[...]
    szb_ref[...] = jax.nn.silu(_dot(xmv, wb3_ref[:, _OFF_ZB:_OFF_ZB + HEADS * DV]))

    zero_halo = jnp.zeros((HALO, D_MODEL), BF16)
    xme_ref[0:HALO] = jnp.where(i > 0, xmp_ref[0], zero_halo)
    xme_ref[HALO:HALO + tile] = xmv
    xme_ref[HALO + tile:] = jnp.where(i < nt - 1, xmn_ref[0], zero_halo)

    def conv_block(j):
        cs = slice(j * cblk, (j + 1) * cblk)
        xe = xme_ref[...]
        hh = _dot(xe, wa4_ref[:, _OFF_H + j * cblk:_OFF_H + (j + 1) * cblk])
        cg = _dot(xe, wa4_ref[:, _OFF_CG + j * cblk:_OFF_CG + (j + 1) * cblk])
        ue_ref[...] = hh * cg
        conv = (cw_ref[0:1, cs] * ue_ref[HALO - 1:HALO - 1 + tile]
                + cw_ref[1:2, cs] * ue_ref[HALO:HALO + tile]
                + cw_ref[2:3, cs] * ue_ref[HALO + 1:HALO + 1 + tile]
                + cb_ref[:, cs])
        bg = _dot(xmv, wa4_ref[:, _OFF_BG + j * cblk:_OFF_BG + (j + 1) * cblk])
        za = _dot(xmv, wa4_ref[:, _OFF_ZA + j * cblk:_OFF_ZA + (j + 1) * cblk])
        ain_ref[:, cs] = (jax.nn.silu(za) * bg * conv).astype(BF16)

    zero_rows = jnp.zeros((DK, LANES), BF16)

    def head_rows(blocks, e):
        return [blk if (r % 2) == e else zero_rows for r, blk in enumerate(blocks)]

    def scores(c, p, slot):
        rows = slice(c * CHUNK, (c + 1) * CHUNK)
        qpb = qr_ref[rows, p * LANES:(p + 1) * LANES].astype(BF16)
        kblk = [kt_ref[0, p * LANES + r * DK:p * LANES + (r + 1) * DK, rows] for r in range(2)]
        for e in range(2):
            h = 2 * p + e
            s = _dot(qpb, jnp.concatenate(head_rows(kblk, e), axis=0)) * dmask_ref[h]
            s_ref[2 * slot + e] = s.astype(BF16)

    def outputs(c, p, slot):
        rows = slice(c * CHUNK, (c + 1) * CHUNK)
        qp = qr_ref[rows, p * LANES:(p + 1) * LANES]
        qfb = jnp.concatenate([(qp * qd_ref[0, p]).astype(BF16),
                               (qp * qd_ref[1, p]).astype(BF16)], axis=1)
        sfp = sf_ref[p].astype(BF16)
        sblk = ([sfp[r * DK:(r + 1) * DK] for r in range(2)]
                + [sb_ref[0, c, p, r * DK:(r + 1) * DK, :] for r in range(2)])
        for e in range(2):
            h = 2 * p + e
            hs = slice(h * DV, (h + 1) * DV)
            states = jnp.concatenate(head_rows(sblk, e), axis=0)
            o = _dot(s_ref[2 * slot + e], v_ref[0, rows, hs]) + _dot(qfb, states)
            mu = jnp.mean(o, axis=-1, keepdims=True)
            d = o - mu
            var = jnp.mean(d * d, axis=-1, keepdims=True)
            rn = d * lax.rsqrt(var + EPS) * gnw_ref[:, hs]
            bin_ref[rows, hs] = (szb_ref[rows, hs] * rn).astype(BF16)
        if p == NPAIR - 1:
            for pp in range(NPAIR):
                sf_ref[pp] = cdp_ref[0, pp] * sf_ref[pp] + kvf_ref[0, c, pp]

    units = [(c, p) for c in range(nchunk) for p in range(NPAIR)]
    per_blk = len(units) // nblk
    for j in range(nblk):
        group = units[j * per_blk:(j + 1) * per_blk]
        for slot, (c, p) in enumerate(group):
            scores(c, p, slot)
        conv_block(j)
        for slot, (c, p) in enumerate(group):
            outputs(c, p, slot)

    for j in range(nblk):
        cs = slice(j * cblk, (j + 1) * cblk)
        pa_ref[:, cs] = jax.nn.sigmoid(
            _dot(xmv, wb3_ref[:, _OFF_GA + j * cblk:_OFF_GA + (j + 1) * cblk]))
        sgb_ref[:, cs] = jax.nn.sigmoid(
            _dot(xmv, wb3_ref[:, _OFF_GB + j * cblk:_OFF_GB + (j + 1) * cblk]))
    pa_ref[...] = pa_ref[...] * _dot(ain_ref[...], wa_ref[...])

    gate = mod_ref[pl.ds(pl.program_id(0), 1), _MOD_GATE:_MOD_GATE + D_MODEL]
    pieces = [slice(r * CHUNK, (r + 1) * CHUNK) for r in range(nchunk)]
    for rs in pieces:
        yb = _dot(bin_ref[rs], wb_ref[...])
        m_ref[rs] = (pa_ref[rs] + sgb_ref[rs] * yb).astype(BF16)
    for rs in pieces:
        z = x_ref[0, rs] + gate * _dot(m_ref[rs], wo_ref[...])
        o_ref[0, rs] = z * _rms_scale(z) * fnw_ref[...]


def _main(x, xm, mod, fnw, w_all, w_a, w_b, w_out, conv_w, conv_b, gn_w,
          kt, v, sb, kvf, kvf_ctx, cdp, cos, sin, dmask, qd, *, tile, cblk):
    b, l, _ = x.shape
    nt = l // tile
    nchunk = tile // CHUNK
    hb = tile // HALO
    nhb = l // HALO
    const = lambda *shape: pl.BlockSpec(shape, lambda bi, i: (0,) * len(shape),
                                        pipeline_mode=pl.Buffered(1))
    w_cols = lambda width, blk: pl.BlockSpec((D_MODEL, width), lambda bi, i: (0, blk),
                                             pipeline_mode=pl.Buffered(1))
    tile_rows = lambda w: pl.BlockSpec((1, tile, w), lambda bi, i: (bi, i, 0))
    chunk_state = pl.BlockSpec((1, nchunk, NPAIR, LANES, DV), lambda bi, i: (bi, i, 0, 0, 0))
    return pl.pallas_call(
        functools.partial(_main_kernel, tile=tile, nt=nt, cblk=cblk),
        grid=(b, nt),
        in_specs=[
            tile_rows(D_MODEL),
            tile_rows(D_MODEL),
            pl.BlockSpec((1, HALO, D_MODEL), lambda bi, i: (bi, jnp.maximum(i * hb - 1, 0), 0)),
            pl.BlockSpec((1, HALO, D_MODEL), lambda bi, i: (bi, jnp.minimum((i + 1) * hb, nhb - 1), 0)),
            const(8, 3 * D_MODEL),
            const(1, D_MODEL),
            w_cols(_A_WIDTH, 0), w_cols(HEADS * DK, _Q_BLOCK), w_cols(_B_WIDTH, _B_BLOCK),
            const(D_MODEL, D_MODEL), const(D_MODEL, D_MODEL), const(D_MODEL, D_MODEL),
            const(3, D_MODEL), const(1, D_MODEL), const(1, D_MODEL),
            pl.BlockSpec((1, HEADS * DK, tile), lambda bi, i: (bi, 0, i)),
            tile_rows(HEADS * DV),
            chunk_state, chunk_state,
            pl.BlockSpec((1, 2, NPAIR, LANES, DV), lambda bi, i: (bi, 0, 0, 0, 0)),
            const(2, NPAIR, LANES, DV),
            pl.BlockSpec((tile, LANES), lambda bi, i: (i, 0)),
            pl.BlockSpec((tile, LANES), lambda bi, i: (i, 0)),
            const(HEADS, CHUNK, CHUNK),
            const(2, NPAIR, CHUNK, LANES),
        ],
        out_specs=tile_rows(D_MODEL),
        out_shape=jax.ShapeDtypeStruct((b, l, D_MODEL), F32),
        scratch_shapes=[
            pltpu.VMEM((NPAIR, LANES, DV), F32),
            pltpu.VMEM((tile + 2 * HALO, D_MODEL), BF16),
            pltpu.VMEM((tile + 2 * HALO, cblk), F32),
            pltpu.VMEM((tile, D_MODEL), BF16),
            pltpu.VMEM((tile, D_MODEL), BF16),
            pltpu.VMEM((tile, D_MODEL), BF16),
            pltpu.VMEM((tile, HEADS * DK), F32),
            pltpu.VMEM((tile, HEADS * DV), F32),
            pltpu.VMEM((tile, D_MODEL), F32),
            pltpu.VMEM((tile, D_MODEL), F32),
            pltpu.VMEM((2 * (nchunk * NPAIR // (D_MODEL // cblk)), CHUNK, CHUNK), BF16),
        ],
        compiler_params=pltpu.CompilerParams(
            dimension_semantics=("arbitrary", "arbitrary"), vmem_limit_bytes=VMEM_LIMIT),
        name="main",
    )(x, xm, xm, xm, mod, fnw, w_all, w_all, w_all, w_a, w_b, w_out, conv_w, conv_b, gn_w,
      kt, v, sb, kvf, kvf_ctx, cdp, cos, sin, dmask, qd)


def _rope_tables(l):
    pos = np.arange(l)
    row = (pos // GRID_W).astype(np.float64)
    col = (pos % GRID_W).astype(np.float64)
    nf = DK // 4
    inv = ROPE_BASE ** (-np.arange(nf, dtype=np.float64) / nf)
    ang = np.concatenate([row[:, None] * inv, col[:, None] * inv], axis=-1)
    cos = np.tile(np.cos(ang), (1, LANES // HALF))
    sin = np.tile(np.sin(ang), (1, LANES // HALF))
    sin = sin * np.where((np.arange(LANES) % DK) < HALF, -1.0, 1.0)
    as_f32 = lambda t: jnp.asarray(t, F32)
    return as_f32(cos), as_f32(sin), as_f32(cos.T * K_SCALE), as_f32(sin.T * K_SCALE)


def kernel(x, c, ctx, c_ctx, norm_w, ada_w, ada_b, w_in, conv_w, conv_b, decay_logit, gn_w,
           w_a, w_b, w_out, final_norm_w):
    b, l, _ = x.shape
    lc = ctx.shape[1]
    assert lc == 2 * CHUNK and l % 1024 == 0 and b < 8
    kv_tile = 1024
    main_tile = 512
    cblk = 256

    mod = _modulation(c, c_ctx.reshape(1, -1), ada_w[0], ada_b[0].reshape(1, -1),
                      norm_w[0].reshape(1, -1))
    dmask, kdt, qd, cdp = _decay_tables(decay_logit[0].astype(F32))

    w_all = w_in[0].astype(BF16)
    wkt = w_all[:, _K_COL:_V_COL].T

    cos, sin, cos_t, sin_t = _rope_tables(l)

    _, _, _, _, kvf_ctx, sb0 = _kv_states(ctx, mod, wkt, w_all, None, kdt, cdp, None,
                                          mod_row=b, tile=lc)
    xm, kt, v, sb, kvf, _ = _kv_states(x, mod, wkt, w_all, (cos_t, sin_t), kdt, cdp, sb0,
                                       mod_row=None, tile=kv_tile)

    return _main(x, xm, mod, final_norm_w.reshape(1, -1), w_all,
                 w_a[0].astype(BF16), w_b[0].astype(BF16), w_out[0].astype(BF16),
                 conv_w[0], conv_b[0].reshape(1, -1), gn_w[0].reshape(1, -1),
                 kt, v, sb, kvf, kvf_ctx, cdp, cos, sin, dmask, qd,
                 tile=main_tile, cblk=cblk)
```

```python
import functools

import numpy as np
import jax
import jax.numpy as jnp
from jax import lax
from jax.experimental import pallas as pl
from jax.experimental.pallas import tpu as pltpu

D_MODEL = 1024
HEADS = 8
DK = 64
DV = 128
CHUNK = 128
GRID_W = 64
ROPE_BASE = 10000.0
EPS = 1e-6
NPAIR = HEADS // 2
LANES = 128
HALF = DK // 2
HALO = 16
K_SCALE = DK ** -0.5
VMEM_LIMIT = 60 * 1024 * 1024

F32 = jnp.float32
BF16 = jnp.bfloat16

_Q_COL, _K_COL, _V_COL = 4096, 4608, 5120
_Q_BLOCK = _Q_COL // (HEADS * DK)
_K_BLOCK = _K_COL // (HEADS * DK)
_V_BLOCK = _V_COL // (HEADS * DV)
_A_WIDTH = 4096
_B_WIDTH = 3072
_B_BLOCK = 6144 // _B_WIDTH
_OFF_H, _OFF_BG, _OFF_CG, _OFF_ZA = 0, 1024, 2048, 3072
_OFF_ZB, _OFF_GA, _OFF_GB = 0, 1024, 2048
_MOD_SHIFT, _MOD_SCALE, _MOD_GATE = 0, D_MODEL, 2 * D_MODEL


def _dot(a, b):
    return jnp.dot(a, b, preferred_element_type=F32)


def _log_sigmoid(x):
    return -(jnp.maximum(-x, 0.0) + jnp.log1p(jnp.exp(-jnp.abs(x))))


def _rms_scale(xv):
    return lax.rsqrt(jnp.mean(xv * xv, axis=-1, keepdims=True) + EPS)


def _mod_kernel(c_ref, cctx_ref, w_ref, b_ref, nw_ref, o_ref, cc_ref):
    j = pl.program_id(0)
    nb = c_ref.shape[0]
    cc_ref[...] = jnp.zeros_like(cc_ref)
    cc_ref[0:nb] = c_ref[...]
    cc_ref[nb:nb + 1] = cctx_ref[...]
    s = jax.nn.silu(cc_ref[...])
    val = jnp.dot(s, w_ref[...], preferred_element_type=F32,
                  precision=lax.Precision.HIGHEST) + b_ref[...]
    o_ref[...] = jnp.where(j == 1, nw_ref[...] * (1.0 + val), val)


def _modulation(c, c_ctx, ada_w, ada_b, norm_w):
    nb = c.shape[0]
    return pl.pallas_call(
        _mod_kernel,
        grid=(3,),
        in_specs=[
            pl.BlockSpec((nb, D_MODEL), lambda j: (0, 0)),
            pl.BlockSpec((1, D_MODEL), lambda j: (0, 0)),
            pl.BlockSpec((D_MODEL, D_MODEL), lambda j: (0, j)),
            pl.BlockSpec((1, D_MODEL), lambda j: (0, j)),
            pl.BlockSpec((1, D_MODEL), lambda j: (0, 0)),
        ],
        out_specs=pl.BlockSpec((8, D_MODEL), lambda j: (0, j)),
        out_shape=jax.ShapeDtypeStruct((8, 3 * D_MODEL), F32),
        scratch_shapes=[pltpu.VMEM((8, D_MODEL), F32)],
        name="mod",
    )(c, c_ctx, ada_w, ada_b, norm_w)


def _decay_kernel(dl_ref, dmask_ref, kdt_ref, qd_ref, cdp_ref):
    c = float(CHUNK)
    ii = lax.broadcasted_iota(jnp.int32, (CHUNK, CHUNK), 0)
    jj = lax.broadcasted_iota(jnp.int32, (CHUNK, CHUNK), 1)
    diff = (ii - jj).astype(F32)
    row = ii.astype(F32)
    col = jj.astype(F32)
    lg = [[_log_sigmoid(jnp.full((1, LANES), dl_ref[d, h], F32)) for h in range(HEADS)]
          for d in range(2)]
    for h in range(HEADS):
        dmask_ref[h] = jnp.where(
            diff > 0.0, jnp.exp(lg[0][h] * jnp.maximum(diff, 0.0)),
            jnp.where(diff < 0.0, jnp.exp(lg[1][h] * jnp.maximum(-diff, 0.0)), 2.0))
    lane_b = jj[0:1, :] >= DK
    row_b = ii >= DK
    for p in range(NPAIR):
        lgf = jnp.where(lane_b, lg[0][2 * p + 1], lg[0][2 * p])
        lgb = jnp.where(lane_b, lg[1][2 * p + 1], lg[1][2 * p])
        qd_ref[0, p] = jnp.exp(lgf * (row + 1.0))
        qd_ref[1, p] = jnp.exp(lgb * (c - row))
        lrf = jnp.where(row_b, lg[0][2 * p + 1], lg[0][2 * p])
        lrb = jnp.where(row_b, lg[1][2 * p + 1], lg[1][2 * p])
        kdt_ref[0, p] = jnp.exp(lrf * (c - 1.0 - col))
        kdt_ref[1, p] = jnp.exp(lrb * col)
        cdp_ref[0, p] = jnp.exp(lrf * c)
        cdp_ref[1, p] = jnp.exp(lrb * c)


def _decay_tables(dl):
    return pl.pallas_call(
        _decay_kernel,
        in_specs=[pl.BlockSpec(memory_space=pltpu.SMEM)],
        out_shape=(
            jax.ShapeDtypeStruct((HEADS, CHUNK, CHUNK), F32),
            jax.ShapeDtypeStruct((2, NPAIR, LANES, CHUNK), F32),
            jax.ShapeDtypeStruct((2, NPAIR, CHUNK, LANES), F32),
            jax.ShapeDtypeStruct((2, NPAIR, LANES, DV), F32),
        ),
        name="decay",
    )(dl)


def _kv_kernel(*refs, rotary, zero_init, mod_row, nchunk):
    refs = list(refs)
    x_ref, mod_ref, wk_ref, wv_ref = refs[:4]
    del refs[:4]
    cos_ref, sin_ref = (refs.pop(0), refs.pop(0)) if rotary else (None, None)
    kdt_ref, cdp_ref = refs.pop(0), refs.pop(0)
    sinit_ref = None if zero_init else refs.pop(0)
    xm_ref, kt_ref, v_ref, sb_ref, kvf_ref, sfin_ref, wkt_ref, wvb_ref = refs
    i = pl.program_id(1)
    mrow = pl.program_id(0) if mod_row is None else mod_row
    shift = mod_ref[pl.ds(mrow, 1), _MOD_SHIFT:_MOD_SHIFT + D_MODEL]
    scale = mod_ref[pl.ds(mrow, 1), _MOD_SCALE:_MOD_SCALE + D_MODEL]

    @pl.when((pl.program_id(0) == 0) & (i == 0))
    def _():
        wkt_ref[...] = wk_ref[...].T.astype(BF16)
        wvb_ref[...] = wv_ref[...].astype(BF16)

    @pl.when(i == 0)
    def _():
        sfin_ref[0] = jnp.zeros_like(sfin_ref[0]) if zero_init else sinit_ref[0]

    row = lax.broadcasted_iota(jnp.int32, (2 * LANES, DV), 0)
    is_a = ((row >> 6) & 1) == 0

    span = 2 * CHUNK

    def project(g):
        tok = slice(g * span, (g + 1) * span)
        x = x_ref[0, tok]
        xm = (x * _rms_scale(x) * scale + shift).astype(BF16)
        xm_ref[0, tok] = xm
        kt = lax.dot_general(wkt_ref[...], xm, (((1,), (1,)), ((), ())),
                             preferred_element_type=F32)
        v = _dot(xm, wvb_ref[...]).astype(BF16)
        v_ref[0, tok] = v
        krs = []
        for p in range(NPAIR):
            kp = kt[p * LANES:(p + 1) * LANES]
            if rotary:
                swapped = jnp.concatenate(
                    [kp[HALF:DK], kp[:HALF], kp[DK + HALF:], kp[DK:DK + HALF]], axis=0)
                kp = kp * cos_ref[:, tok] + swapped * sin_ref[:, tok]
            else:
                kp = kp * K_SCALE
            krs.append(kp)
            kt_ref[0, p * LANES:(p + 1) * LANES, tok] = kp.astype(BF16)
        return krs, v

    def advance(g, krs, v):
        for c in (1, 0):
            tok = slice(c * CHUNK, (c + 1) * CHUNK)
            for p in range(NPAIR):
                kc = krs[p][:, tok]
                kd = jnp.concatenate([kc * kdt_ref[0, p], kc * kdt_ref[1, p]], axis=0).astype(BF16)
                both = _dot(kd, v[tok, 2 * p * DV:2 * (p + 1) * DV])
                inc = jnp.where(is_a, both[:, :DV], both[:, DV:])
                sb_ref[0, 2 * g + c, p] = sfin_ref[0, p].astype(BF16)
                kvf_ref[0, 2 * g + c, p] = inc[:LANES]
                sfin_ref[0, p] = cdp_ref[1, p] * sfin_ref[0, p] + inc[LANES:]

    pending = None
    for g in reversed(range(nchunk // 2)):
        done = project(g)
        if pending is not None:
            advance(*pending)
        pending = (g,) + done
    advance(*pending)


def _kv_states(xs, mod, w, rope_t, kdt, cdp, sinit, *, mod_row, tile):
    b, l, _ = xs.shape
    nt = l // tile
    nchunk = tile // CHUNK
    nc = l // CHUNK
    rotary = rope_t is not None
    zero_init = sinit is None
    rev = lambda bi, i: (bi, nt - 1 - i, 0)
    const = lambda *shape: pl.BlockSpec(shape, lambda bi, i: (0,) * len(shape))
    state_blk = pl.BlockSpec((1, nchunk, NPAIR, LANES, DV), lambda bi, i: (bi, nt - 1 - i, 0, 0, 0))
    pair_state = pl.BlockSpec((1, NPAIR, LANES, DV), lambda bi, i: (bi, 0, 0, 0))
    in_specs = [
        pl.BlockSpec((1, tile, D_MODEL), rev),
        const(8, 3 * D_MODEL),
        pl.BlockSpec((D_MODEL, HEADS * DK), lambda bi, i: (0, _K_BLOCK), pipeline_mode=pl.Buffered(1)),
        pl.BlockSpec((D_MODEL, HEADS * DV), lambda bi, i: (0, _V_BLOCK), pipeline_mode=pl.Buffered(1)),
    ]
    args = [xs, mod, w, w]
    if rotary:
        in_specs += [pl.BlockSpec((LANES, tile), lambda bi, i: (0, nt - 1 - i))] * 2
        args += list(rope_t)
    in_specs += [const(2, NPAIR, LANES, CHUNK), const(2, NPAIR, LANES, DV)]
    args += [kdt, cdp]
    if not zero_init:
        in_specs.append(pair_state)
        args.append(sinit)
    return pl.pallas_call(
        functools.partial(_kv_kernel, rotary=rotary, zero_init=zero_init, mod_row=mod_row,
                          nchunk=nchunk),
        grid=(b, nt),
        in_specs=in_specs,
        out_specs=(
            pl.BlockSpec((1, tile, D_MODEL), rev),
            pl.BlockSpec((1, HEADS * DK, tile), lambda bi, i: (bi, 0, nt - 1 - i)),
            pl.BlockSpec((1, tile, HEADS * DV), rev),
            state_blk,
            state_blk,
            pair_state,
        ),
        out_shape=(
            jax.ShapeDtypeStruct((b, l, D_MODEL), BF16),
            jax.ShapeDtypeStruct((b, HEADS * DK, l), BF16),
            jax.ShapeDtypeStruct((b, l, HEADS * DV), BF16),
            jax.ShapeDtypeStruct((b, nc, NPAIR, LANES, DV), BF16),
            jax.ShapeDtypeStruct((b, nc, NPAIR, LANES, DV), F32),
            jax.ShapeDtypeStruct((b, NPAIR, LANES, DV), F32),
        ),
        scratch_shapes=[
            pltpu.VMEM((HEADS * DK, D_MODEL), BF16),
            pltpu.VMEM((D_MODEL, HEADS * DV), BF16),
        ],
        compiler_params=pltpu.CompilerParams(
            dimension_semantics=("arbitrary", "arbitrary"), vmem_limit_bytes=VMEM_LIMIT),
        name="kv_rot" if rotary else "kv_ctx",
    )(*args)


_W_CHUNK = 512
_IN_CHUNKS = ([("a", k) for k in range(_A_WIDTH // _W_CHUNK)] + [("q", 0)]
              + [("b", k) for k in range(_B_WIDTH // _W_CHUNK)])
_SQ_CHUNKS = D_MODEL // _W_CHUNK
_N_CONV = len(_IN_CHUNKS) + 3 * _SQ_CHUNKS


def _main_kernel(x_ref, xm_ref, xmp_ref, xmn_ref, mod_ref, fnw_ref,
                 win_ref, waf_ref, wbf_ref, wof_ref, cw_ref, cb_ref, gnw_ref,
                 kt_ref, v_ref, sb_ref, kvf_ref, kvfc_ref, cdp_ref,
                 cos_ref, sin_ref, dmask_ref, qd_ref,
                 o_ref,
                 wa4_ref, wq_ref, wb3_ref, wa_ref, wb_ref, wo_ref,
                 *scratch, tile, nt, cblk):
    s = pl.program_id(0)
    dst_of = {"a": wa4_ref, "q": wq_ref, "b": wb3_ref}
    for step, (name, k) in enumerate(_IN_CHUNKS):
        @pl.when(s == step)
        def _(name=name, k=k):
            dst_of[name][:, k * _W_CHUNK:(k + 1) * _W_CHUNK] = win_ref[...].astype(BF16)
    for j, (src, dst) in enumerate(((waf_ref, wa_ref), (wbf_ref, wb_ref), (wof_ref, wo_ref))):
        for k in range(_SQ_CHUNKS):
            @pl.when(s == len(_IN_CHUNKS) + j * _SQ_CHUNKS + k)
            def _(src=src, dst=dst, k=k):
                dst[:, k * _W_CHUNK:(k + 1) * _W_CHUNK] = src[...].astype(BF16)

    @pl.when(s >= _N_CONV)
    def _():
        t = s - _N_CONV
        _tile_step(t // nt, t % nt, x_ref, xm_ref, xmp_ref, xmn_ref, mod_ref, fnw_ref,
                   wa4_ref, wq_ref, wb3_ref, wa_ref, wb_ref, wo_ref, cw_ref, cb_ref, gnw_ref,
                   kt_ref, v_ref, sb_ref, kvf_ref, kvfc_ref, cdp_ref,
                   cos_ref, sin_ref, dmask_ref, qd_ref, o_ref, *scratch,
                   tile=tile, nt=nt, cblk=cblk)


def _tile_step(bi, i, x_ref, xm_ref, xmp_ref, xmn_ref, mod_ref, fnw_ref,
               wa4_ref, wq_ref, wb3_ref, wa_ref, wb_ref, wo_ref, cw_ref, cb_ref, gnw_ref,
               kt_ref, v_ref, sb_ref, kvf_ref, kvfc_ref, cdp_ref,
               cos_ref, sin_ref, dmask_ref, qd_ref,
               o_ref,
               sf_ref, xme_ref, ue_ref, ain_ref, bin_ref, m_ref, qr_ref, szb_ref, pa_ref, sgb_ref,
               s_ref, *, tile, nt, cblk):
    nchunk = tile // CHUNK
    nblk = D_MODEL // cblk

    @pl.when(i == 0)
    def _():
        for p in range(NPAIR):
            sf_ref[p] = cdp_ref[0, p] * kvfc_ref[0, 0, p] + kvfc_ref[0, 1, p]

    xmv = xm_ref[0]

    q = _dot(xmv, wq_ref[...])
    first_half = (lax.broadcasted_iota(jnp.int32, (1, LANES), 1) & HALF) == 0
    for p in range(NPAIR):
        qp = q[:, p * LANES:(p + 1) * LANES]
        partner = jnp.where(first_half, pltpu.roll(qp, LANES - HALF, 1), pltpu.roll(qp, HALF, 1))
        qr_ref[:, p * LANES:(p + 1) * LANES] = qp * cos_ref[...] + partner * sin_ref[...]
    szb_ref[...] = jax.nn.silu(_dot(xmv, wb3_ref[:, _OFF_ZB:_OFF_ZB + HEADS * DV]))

    zero_halo = jnp.zeros((HALO, D_MODEL), BF16)
    xme_ref[0:HALO] = jnp.where(i > 0, xmp_ref[0], zero_halo)
    xme_ref[HALO:HALO + tile] = xmv
    xme_ref[HALO + tile:] = jnp.where(i < nt - 1, xmn_ref[0], zero_halo)

    def conv_block(j):
        cs = slice(j * cblk, (j + 1) * cblk)
        xe = xme_ref[...]
        hh = _dot(xe, wa4_ref[:, _OFF_H + j * cblk:_OFF_H + (j + 1) * cblk])
        cg = _dot(xe, wa4_ref[:, _OFF_CG + j * cblk:_OFF_CG + (j + 1) * cblk])
        ue_ref[...] = hh * cg
        conv = (cw_ref[0:1, cs] * ue_ref[HALO - 1:HALO - 1 + tile]
                + cw_ref[1:2, cs] * ue_ref[HALO:HALO + tile]
                + cw_ref[2:3, cs] * ue_ref[HALO + 1:HALO + 1 + tile]
                + cb_ref[:, cs])
        bg = _dot(xmv, wa4_ref[:, _OFF_BG + j * cblk:_OFF_BG + (j + 1) * cblk])
        za = _dot(xmv, wa4_ref[:, _OFF_ZA + j * cblk:_OFF_ZA + (j + 1) * cblk])
        ain_ref[:, cs] = (jax.nn.silu(za) * bg * conv).astype(BF16)

    zero_rows = jnp.zeros((DK, LANES), BF16)

    def head_rows(blocks, e):
        return [blk if (r % 2) == e else zero_rows for r, blk in enumerate(blocks)]

    def scores(c, p, slot):
        rows = slice(c * CHUNK, (c + 1) * CHUNK)
        qpb = qr_ref[rows, p * LANES:(p + 1) * LANES].astype(BF16)
        kblk = [kt_ref[0, p * LANES + r * DK:p * LANES + (r + 1) * DK, rows] for r in range(2)]
        for e in range(2):
            h = 2 * p + e
            s = _dot(qpb, jnp.concatenate(head_rows(kblk, e), axis=0)) * dmask_ref[h]
            s_ref[2 * slot + e] = s.astype(BF16)

    def outputs(c, p, slot):
        rows = slice(c * CHUNK, (c + 1) * CHUNK)
        qp = qr_ref[rows, p * LANES:(p + 1) * LANES]
        qfb = jnp.concatenate([(qp * qd_ref[0, p]).astype(BF16),
                               (qp * qd_ref[1, p]).astype(BF16)], axis=1)
        sfp = sf_ref[p].astype(BF16)
        sblk = ([sfp[r * DK:(r + 1) * DK] for r in range(2)]
                + [sb_ref[0, c, p, r * DK:(r + 1) * DK, :] for r in range(2)])
        for e in range(2):
            h = 2 * p + e
            hs = slice(h * DV, (h + 1) * DV)
            states = jnp.concatenate(head_rows(sblk, e), axis=0)
            o = _dot(s_ref[2 * slot + e], v_ref[0, rows, hs]) + _dot(qfb, states)
            mu = jnp.mean(o, axis=-1, keepdims=True)
            d = o - mu
            var = jnp.mean(d * d, axis=-1, keepdims=True)
            rn = d * lax.rsqrt(var + EPS) * gnw_ref[:, hs]
            bin_ref[rows, hs] = (szb_ref[rows, hs] * rn).astype(BF16)
        if p == NPAIR - 1:
            for pp in range(NPAIR):
                sf_ref[pp] = cdp_ref[0, pp] * sf_ref[pp] + kvf_ref[0, c, pp]

    units = [(c, p) for c in range(nchunk) for p in range(NPAIR)]
    per_blk = len(units) // nblk
    for j in range(nblk):
        group = units[j * per_blk:(j + 1) * per_blk]
        for slot, (c, p) in enumerate(group):
            scores(c, p, slot)
        conv_block(j)
        for slot, (c, p) in enumerate(group):
            outputs(c, p, slot)

    for j in range(nblk):
        cs = slice(j * cblk, (j + 1) * cblk)
        pa_ref[:, cs] = jax.nn.sigmoid(
            _dot(xmv, wb3_ref[:, _OFF_GA + j * cblk:_OFF_GA + (j + 1) * cblk]))
        sgb_ref[:, cs] = jax.nn.sigmoid(
            _dot(xmv, wb3_ref[:, _OFF_GB + j * cblk:_OFF_GB + (j + 1) * cblk]))
    pa_ref[...] = pa_ref[...] * _dot(ain_ref[...], wa_ref[...])

    gate = mod_ref[pl.ds(bi, 1), _MOD_GATE:_MOD_GATE + D_MODEL]
    pieces = [slice(r * CHUNK, (r + 1) * CHUNK) for r in range(nchunk)]
    for rs in pieces:
        yb = _dot(bin_ref[rs], wb_ref[...])
        m_ref[rs] = (pa_ref[rs] + sgb_ref[rs] * yb).astype(BF16)
    for rs in pieces:
        z = x_ref[0, rs] + gate * _dot(m_ref[rs], wo_ref[...])
        o_ref[0, rs] = z * _rms_scale(z) * fnw_ref[...]


def _main(x, xm, mod, fnw, w, w_a, w_b, w_out, conv_w, conv_b, gn_w,
          kt, v, sb, kvf, kvf_ctx, cdp, cos, sin, dmask, qd, *, tile, cblk):
    b, l, _ = x.shape
    nt = l // tile
    nchunk = tile // CHUNK
    hb = tile // HALO
    nhb = l // HALO

    def tile_of(s):
        t = jnp.maximum(s - _N_CONV, 0)
        return t // nt, t % nt

    def in_chunk(s):
        k = jnp.minimum(s, len(_IN_CHUNKS) - 1)
        b_start = _A_WIDTH // _W_CHUNK + 1
        return 0, jnp.where(k >= b_start, k - b_start + _B_BLOCK * _B_WIDTH // _W_CHUNK, k)

    def sq_chunk(j):
        first = len(_IN_CHUNKS) + j * _SQ_CHUNKS
        return lambda s: (0, jnp.clip(s - first, 0, _SQ_CHUNKS - 1))

    const = lambda *shape: pl.BlockSpec(shape, lambda s: (0,) * len(shape),
                                        pipeline_mode=pl.Buffered(1))
    w_chunk = lambda index_map, **kw: pl.BlockSpec((D_MODEL, _W_CHUNK), index_map, **kw)
    single = dict(pipeline_mode=pl.Buffered(1))
    tile_rows = lambda width: pl.BlockSpec((1, tile, width), lambda s: (*tile_of(s), 0))
    chunk_state = pl.BlockSpec((1, nchunk, NPAIR, LANES, DV), lambda s: (*tile_of(s), 0, 0, 0))
    rope_rows = pl.BlockSpec((tile, LANES), lambda s: (tile_of(s)[1], 0))
    return pl.pallas_call(
        functools.partial(_main_kernel, tile=tile, nt=nt, cblk=cblk),
        grid=(_N_CONV + b * nt,),
        in_specs=[
            tile_rows(D_MODEL),
            tile_rows(D_MODEL),
            pl.BlockSpec((1, HALO, D_MODEL),
                         lambda s: (tile_of(s)[0], jnp.maximum(tile_of(s)[1] * hb - 1, 0), 0)),
            pl.BlockSpec((1, HALO, D_MODEL),
                         lambda s: (tile_of(s)[0], jnp.minimum((tile_of(s)[1] + 1) * hb, nhb - 1), 0)),
            const(8, 3 * D_MODEL),
            const(1, D_MODEL),
            w_chunk(in_chunk),
            w_chunk(sq_chunk(0), **single), w_chunk(sq_chunk(1), **single), w_chunk(sq_chunk(2), **single),
            const(3, D_MODEL), const(1, D_MODEL), const(1, D_MODEL),
            pl.BlockSpec((1, HEADS * DK, tile), lambda s: (tile_of(s)[0], 0, tile_of(s)[1])),
            tile_rows(HEADS * DV),
            chunk_state, chunk_state,
            pl.BlockSpec((1, 2, NPAIR, LANES, DV), lambda s: (tile_of(s)[0], 0, 0, 0, 0)),
            const(2, NPAIR, LANES, DV),
            rope_rows, rope_rows,
            const(HEADS, CHUNK, CHUNK),
            const(2, NPAIR, CHUNK, LANES),
        ],
        out_specs=tile_rows(D_MODEL),
        out_shape=jax.ShapeDtypeStruct((b, l, D_MODEL), F32),
        scratch_shapes=[
            pltpu.VMEM((D_MODEL, _A_WIDTH), BF16),
            pltpu.VMEM((D_MODEL, HEADS * DK), BF16),
            pltpu.VMEM((D_MODEL, _B_WIDTH), BF16),
            pltpu.VMEM((D_MODEL, D_MODEL), BF16),
            pltpu.VMEM((D_MODEL, D_MODEL), BF16),
            pltpu.VMEM((D_MODEL, D_MODEL), BF16),
            pltpu.VMEM((NPAIR, LANES, DV), F32),
            pltpu.VMEM((tile + 2 * HALO, D_MODEL), BF16),
            pltpu.VMEM((tile + 2 * HALO, cblk), F32),
            pltpu.VMEM((tile, D_MODEL), BF16),
            pltpu.VMEM((tile, D_MODEL), BF16),
            pltpu.VMEM((tile, D_MODEL), BF16),
            pltpu.VMEM((tile, HEADS * DK), F32),
            pltpu.VMEM((tile, HEADS * DV), F32),
            pltpu.VMEM((tile, D_MODEL), F32),
            pltpu.VMEM((tile, D_MODEL), F32),
            pltpu.VMEM((2 * (nchunk * NPAIR // (D_MODEL // cblk)), CHUNK, CHUNK), BF16),
        ],
        compiler_params=pltpu.CompilerParams(
            dimension_semantics=("arbitrary",), vmem_limit_bytes=VMEM_LIMIT),
        name="main",
    )(x, xm, xm, xm, mod, fnw, w, w_a, w_b, w_out, conv_w, conv_b, gn_w,
      kt, v, sb, kvf, kvf_ctx, cdp, cos, sin, dmask, qd)


def _rope_tables(l):
    pos = np.arange(l)
    row = (pos // GRID_W).astype(np.float64)
    col = (pos % GRID_W).astype(np.float64)
    nf = DK // 4
    inv = ROPE_BASE ** (-np.arange(nf, dtype=np.float64) / nf)
    ang = np.concatenate([row[:, None] * inv, col[:, None] * inv], axis=-1)
    cos = np.tile(np.cos(ang), (1, LANES // HALF))
    sin = np.tile(np.sin(ang), (1, LANES // HALF))
    sin = sin * np.where((np.arange(LANES) % DK) < HALF, -1.0, 1.0)
    as_f32 = lambda t: jnp.asarray(t, F32)
    return as_f32(cos), as_f32(sin), as_f32(cos.T * K_SCALE), as_f32(sin.T * K_SCALE)


def kernel(x, c, ctx, c_ctx, norm_w, ada_w, ada_b, w_in, conv_w, conv_b, decay_logit, gn_w,
           w_a, w_b, w_out, final_norm_w):
    b, l, _ = x.shape
    lc = ctx.shape[1]
    assert lc == 2 * CHUNK and l % 1024 == 0 and b < 8
    kv_tile = 1024
    main_tile = 512
    cblk = 256

    mod = _modulation(c, c_ctx.reshape(1, -1), ada_w[0], ada_b[0].reshape(1, -1),
                      norm_w[0].reshape(1, -1))
    dmask, kdt, qd, cdp = _decay_tables(decay_logit[0].astype(F32))

    cos, sin, cos_t, sin_t = _rope_tables(l)

    _, _, _, _, kvf_ctx, sb0 = _kv_states(ctx, mod, w_in[0], None, kdt, cdp, None,
                                          mod_row=b, tile=lc)
    xm, kt, v, sb, kvf, _ = _kv_states(x, mod, w_in[0], (cos_t, sin_t), kdt, cdp, sb0,
                                       mod_row=None, tile=kv_tile)

    return _main(x, xm, mod, final_norm_w.reshape(1, -1), w_in[0], w_a[0], w_b[0], w_out[0],
                 conv_w[0], conv_b[0].reshape(1, -1), gn_w[0].reshape(1, -1),
                 kt, v, sb, kvf, kvf_ctx, cdp, cos, sin, dmask, qd,
                 tile=main_tile, cblk=cblk)
```

```python
import functools

import numpy as np
import jax
import jax.numpy as jnp
from jax import lax
from jax.experimental import pallas as pl
from jax.experimental.pallas import tpu as pltpu

D_MODEL = 1024
HEADS = 8
DK = 64
DV = 128
CHUNK = 128
GRID_W = 64
ROPE_BASE = 10000.0
EPS = 1e-6
NPAIR = HEADS // 2
LANES = 128
HALF = DK // 2
HALO = 16
K_SCALE = DK ** -0.5
VMEM_LIMIT = 58 * 1024 * 1024

F32 = jnp.float32
BF16 = jnp.bfloat16

_Q_COL, _K_COL, _V_COL = 4096, 4608, 5120
_Q_BLOCK = _Q_COL // (HEADS * DK)
_K_BLOCK = _K_COL // (HEADS * DK)
_V_BLOCK = _V_COL // (HEADS * DV)
_A_WIDTH = 4096
_B_WIDTH = 3072
_B_BLOCK = 6144 // _B_WIDTH
_OFF_H, _OFF_BG, _OFF_CG, _OFF_ZA = 0, 1024, 2048, 3072
_OFF_ZB, _OFF_GA, _OFF_GB = 0, 1024, 2048
_MOD_SHIFT, _MOD_SCALE, _MOD_GATE = 0, D_MODEL, 2 * D_MODEL


def _dot(a, b):
    return jnp.dot(a, b, preferred_element_type=F32)


def _log_sigmoid(x):
    return -(jnp.maximum(-x, 0.0) + jnp.log1p(jnp.exp(-jnp.abs(x))))


def _rms_scale(xv):
    return lax.rsqrt(jnp.mean(xv * xv, axis=-1, keepdims=True) + EPS)


def _mod_kernel(c_ref, cctx_ref, w_ref, b_ref, nw_ref, o_ref, cc_ref):
    j = pl.program_id(0)
    nb = c_ref.shape[0]
    cc_ref[...] = jnp.zeros_like(cc_ref)
    cc_ref[0:nb] = c_ref[...]
    cc_ref[nb:nb + 1] = cctx_ref[...]
    s = jax.nn.silu(cc_ref[...])
    val = jnp.dot(s, w_ref[...], preferred_element_type=F32,
                  precision=lax.Precision.HIGHEST) + b_ref[...]
    o_ref[...] = jnp.where(j == 1, nw_ref[...] * (1.0 + val), val)


def _modulation(c, c_ctx, ada_w, ada_b, norm_w):
    nb = c.shape[0]
    return pl.pallas_call(
        _mod_kernel,
        grid=(3,),
        in_specs=[
            pl.BlockSpec((nb, D_MODEL), lambda j: (0, 0)),
            pl.BlockSpec((1, D_MODEL), lambda j: (0, 0)),
            pl.BlockSpec((D_MODEL, D_MODEL), lambda j: (0, j)),
            pl.BlockSpec((1, D_MODEL), lambda j: (0, j)),
            pl.BlockSpec((1, D_MODEL), lambda j: (0, 0)),
        ],
        out_specs=pl.BlockSpec((8, D_MODEL), lambda j: (0, j)),
        out_shape=jax.ShapeDtypeStruct((8, 3 * D_MODEL), F32),
        scratch_shapes=[pltpu.VMEM((8, D_MODEL), F32)],
        name="mod",
    )(c, c_ctx, ada_w, ada_b, norm_w)


def _decay_kernel(dl_ref, dmask_ref, kdt_ref, qd_ref, cdp_ref):
    c = float(CHUNK)
    ii = lax.broadcasted_iota(jnp.int32, (CHUNK, CHUNK), 0)
    jj = lax.broadcasted_iota(jnp.int32, (CHUNK, CHUNK), 1)
    diff = (ii - jj).astype(F32)
    row = ii.astype(F32)
    col = jj.astype(F32)
    lg = [[_log_sigmoid(jnp.full((1, LANES), dl_ref[d, h], F32)) for h in range(HEADS)]
          for d in range(2)]
    for h in range(HEADS):
        dmask_ref[h] = jnp.where(
            diff > 0.0, jnp.exp(lg[0][h] * jnp.maximum(diff, 0.0)),
            jnp.where(diff < 0.0, jnp.exp(lg[1][h] * jnp.maximum(-diff, 0.0)), 2.0))
    lane_b = jj[0:1, :] >= DK
    row_b = ii >= DK
    for p in range(NPAIR):
        lgf = jnp.where(lane_b, lg[0][2 * p + 1], lg[0][2 * p])
        lgb = jnp.where(lane_b, lg[1][2 * p + 1], lg[1][2 * p])
        qd_ref[0, p] = jnp.exp(lgf * (row + 1.0))
        qd_ref[1, p] = jnp.exp(lgb * (c - row))
        lrf = jnp.where(row_b, lg[0][2 * p + 1], lg[0][2 * p])
        lrb = jnp.where(row_b, lg[1][2 * p + 1], lg[1][2 * p])
        kdt_ref[0, p] = jnp.exp(lrf * (c - 1.0 - col))
        kdt_ref[1, p] = jnp.exp(lrb * col)
        cdp_ref[0, p] = jnp.exp(lrf * c)
        cdp_ref[1, p] = jnp.exp(lrb * c)


def _decay_tables(dl):
    return pl.pallas_call(
        _decay_kernel,
        in_specs=[pl.BlockSpec(memory_space=pltpu.SMEM)],
        out_shape=(
            jax.ShapeDtypeStruct((HEADS, CHUNK, CHUNK), F32),
            jax.ShapeDtypeStruct((2, NPAIR, LANES, CHUNK), F32),
            jax.ShapeDtypeStruct((2, NPAIR, CHUNK, LANES), F32),
            jax.ShapeDtypeStruct((2, NPAIR, LANES, DV), F32),
        ),
        name="decay",
    )(dl)


def _kv_kernel(*refs, rotary, zero_init, mod_row, nchunk):
    refs = list(refs)
    x_ref, mod_ref, wk_ref, wv_ref = refs[:4]
    del refs[:4]
    cos_ref, sin_ref = (refs.pop(0), refs.pop(0)) if rotary else (None, None)
    kdt_ref, cdp_ref = refs.pop(0), refs.pop(0)
    sinit_ref = None if zero_init else refs.pop(0)
    xm_ref, kt_ref, v_ref, sb_ref, kvf_ref, sfin_ref, wkt_ref, wvb_ref = refs
    i = pl.program_id(1)
    mrow = pl.program_id(0) if mod_row is None else mod_row
    shift = mod_ref[pl.ds(mrow, 1), _MOD_SHIFT:_MOD_SHIFT + D_MODEL]
    scale = mod_ref[pl.ds(mrow, 1), _MOD_SCALE:_MOD_SCALE + D_MODEL]

    @pl.when((pl.program_id(0) == 0) & (i == 0))
    def _():
        wkt_ref[...] = wk_ref[...].T.astype(BF16)
        wvb_ref[...] = wv_ref[...].astype(BF16)

    @pl.when(i == 0)
    def _():
        sfin_ref[0] = jnp.zeros_like(sfin_ref[0]) if zero_init else sinit_ref[0]

    row = lax.broadcasted_iota(jnp.int32, (2 * LANES, DV), 0)
    is_a = ((row >> 6) & 1) == 0

    span = 2 * CHUNK

    def project(g):
        tok = slice(g * span, (g + 1) * span)
        x = x_ref[0, tok]
        xm = (x * _rms_scale(x) * scale + shift).astype(BF16)
        xm_ref[0, tok] = xm
        kt = lax.dot_general(wkt_ref[...], xm, (((1,), (1,)), ((), ())),
                             preferred_element_type=F32)
        v = _dot(xm, wvb_ref[...]).astype(BF16)
        v_ref[0, tok] = v
        krs = []
        for p in range(NPAIR):
            kp = kt[p * LANES:(p + 1) * LANES]
            if rotary:
                swapped = jnp.concatenate(
                    [kp[HALF:DK], kp[:HALF], kp[DK + HALF:], kp[DK:DK + HALF]], axis=0)
                kp = kp * cos_ref[:, tok] + swapped * sin_ref[:, tok]
            else:
                kp = kp * K_SCALE
            krs.append(kp)
            kt_ref[0, p * LANES:(p + 1) * LANES, tok] = kp.astype(BF16)
        return krs, v

    def advance(g, krs, v):
        for c in (1, 0):
            tok = slice(c * CHUNK, (c + 1) * CHUNK)
            for p in range(NPAIR):
                kc = krs[p][:, tok]
                kd = jnp.concatenate([kc * kdt_ref[0, p], kc * kdt_ref[1, p]], axis=0).astype(BF16)
                both = _dot(kd, v[tok, 2 * p * DV:2 * (p + 1) * DV])
                inc = jnp.where(is_a, both[:, :DV], both[:, DV:])
                sb_ref[0, 2 * g + c, p] = sfin_ref[0, p].astype(BF16)
                kvf_ref[0, 2 * g + c, p] = inc[:LANES]
                sfin_ref[0, p] = cdp_ref[1, p] * sfin_ref[0, p] + inc[LANES:]

    pending = None
    for g in reversed(range(nchunk // 2)):
        done = project(g)
        if pending is not None:
            advance(*pending)
        pending = (g,) + done
    advance(*pending)


def _kv_states(xs, mod, w, rope_t, kdt, cdp, sinit, *, mod_row, tile):
    b, l, _ = xs.shape
    nt = l // tile
    nchunk = tile // CHUNK
    nc = l // CHUNK
    rotary = rope_t is not None
    zero_init = sinit is None
    rev = lambda bi, i: (bi, nt - 1 - i, 0)
    const = lambda *shape: pl.BlockSpec(shape, lambda bi, i: (0,) * len(shape))
    state_blk = pl.BlockSpec((1, nchunk, NPAIR, LANES, DV), lambda bi, i: (bi, nt - 1 - i, 0, 0, 0))
    pair_state = pl.BlockSpec((1, NPAIR, LANES, DV), lambda bi, i: (bi, 0, 0, 0))
    in_specs = [
        pl.BlockSpec((1, tile, D_MODEL), rev),
        const(8, 3 * D_MODEL),
        pl.BlockSpec((D_MODEL, HEADS * DK), lambda bi, i: (0, _K_BLOCK), pipeline_mode=pl.Buffered(1)),
        pl.BlockSpec((D_MODEL, HEADS * DV), lambda bi, i: (0, _V_BLOCK), pipeline_mode=pl.Buffered(1)),
    ]
    args = [xs, mod, w, w]
    if rotary:
        in_specs += [pl.BlockSpec((LANES, tile), lambda bi, i: (0, nt - 1 - i))] * 2
        args += list(rope_t)
    in_specs += [const(2, NPAIR, LANES, CHUNK), const(2, NPAIR, LANES, DV)]
    args += [kdt, cdp]
    if not zero_init:
        in_specs.append(pair_state)
        args.append(sinit)
    return pl.pallas_call(
        functools.partial(_kv_kernel, rotary=rotary, zero_init=zero_init, mod_row=mod_row,
                          nchunk=nchunk),
        grid=(b, nt),
        in_specs=in_specs,
        out_specs=(
            pl.BlockSpec((1, tile, D_MODEL), rev),
            pl.BlockSpec((1, HEADS * DK, tile), lambda bi, i: (bi, 0, nt - 1 - i)),
            pl.BlockSpec((1, tile, HEADS * DV), rev),
            state_blk,
            state_blk,
            pair_state,
        ),
        out_shape=(
            jax.ShapeDtypeStruct((b, l, D_MODEL), BF16),
            jax.ShapeDtypeStruct((b, HEADS * DK, l), BF16),
            jax.ShapeDtypeStruct((b, l, HEADS * DV), BF16),
            jax.ShapeDtypeStruct((b, nc, NPAIR, LANES, DV), BF16),
            jax.ShapeDtypeStruct((b, nc, NPAIR, LANES, DV), F32),
            jax.ShapeDtypeStruct((b, NPAIR, LANES, DV), F32),
        ),
        scratch_shapes=[
            pltpu.VMEM((HEADS * DK, D_MODEL), BF16),
            pltpu.VMEM((D_MODEL, HEADS * DV), BF16),
        ],
        compiler_params=pltpu.CompilerParams(
            dimension_semantics=("arbitrary", "arbitrary"), vmem_limit_bytes=VMEM_LIMIT),
        name="kv_rot" if rotary else "kv_ctx",
    )(*args)


_W_ROWS = 64
_N_CONV = D_MODEL // _W_ROWS
_IN_W = 9216


def _main_kernel(x_ref, xm_ref, xmp_ref, xmn_ref, mod_ref, fnw_ref,
                 win_ref, waf_ref, wbf_ref, wof_ref, cw_ref, cb_ref, gnw_ref,
                 kt_ref, v_ref, sb_ref, kvf_ref, kvfc_ref, cdp_ref,
                 cos_ref, sin_ref, dmask_ref, qd_ref,
                 o_ref,
                 wa4_ref, wq_ref, wb3_ref, wa_ref, wb_ref, wo_ref,
                 *scratch, tile, nt, cblk):
    s = pl.program_id(0)

    @pl.when(s < _N_CONV)
    def _():
        rows = pl.ds(pl.multiple_of(s * _W_ROWS, _W_ROWS), _W_ROWS)
        wa4_ref[rows, :] = win_ref[:, 0:_A_WIDTH].astype(BF16)
        wq_ref[rows, :] = win_ref[:, _Q_COL:_K_COL].astype(BF16)
        wb3_ref[rows, :] = win_ref[:, _B_BLOCK * _B_WIDTH:(_B_BLOCK + 1) * _B_WIDTH].astype(BF16)
        wa_ref[rows, :] = waf_ref[...].astype(BF16)
        wb_ref[rows, :] = wbf_ref[...].astype(BF16)
        wo_ref[rows, :] = wof_ref[...].astype(BF16)

    @pl.when(s >= _N_CONV)
    def _():
        t = s - _N_CONV
        _tile_step(t // nt, t % nt, x_ref, xm_ref, xmp_ref, xmn_ref, mod_ref, fnw_ref,
                   wa4_ref, wq_ref, wb3_ref, wa_ref, wb_ref, wo_ref, cw_ref, cb_ref, gnw_ref,
                   kt_ref, v_ref, sb_ref, kvf_ref, kvfc_ref, cdp_ref,
                   cos_ref, sin_ref, dmask_ref, qd_ref, o_ref, *scratch,
                   tile=tile, nt=nt, cblk=cblk)


def _tile_step(bi, i, x_ref, xm_ref, xmp_ref, xmn_ref, mod_ref, fnw_ref,
               wa4_ref, wq_ref, wb3_ref, wa_ref, wb_ref, wo_ref, cw_ref, cb_ref, gnw_ref,
               kt_ref, v_ref, sb_ref, kvf_ref, kvfc_ref, cdp_ref,
               cos_ref, sin_ref, dmask_ref, qd_ref,
               o_ref,
               sf_ref, xme_ref, ue_ref, ain_ref, bin_ref, m_ref, qr_ref, szb_ref, pa_ref, sgb_ref,
               s_ref, *, tile, nt, cblk):
    nchunk = tile // CHUNK
    nblk = D_MODEL // cblk

    @pl.when(i == 0)
    def _():
        for p in range(NPAIR):
            sf_ref[p] = cdp_ref[0, p] * kvfc_ref[0, 0, p] + kvfc_ref[0, 1, p]

    xmv = xm_ref[0]

    q = _dot(xmv, wq_ref[...])
    first_half = (lax.broadcasted_iota(jnp.int32, (1, LANES), 1) & HALF) == 0
    for p in range(NPAIR):
        qp = q[:, p * LANES:(p + 1) * LANES]
        partner = jnp.where(first_half, pltpu.roll(qp, LANES - HALF, 1), pltpu.roll(qp, HALF, 1))
        qr_ref[:, p * LANES:(p + 1) * LANES] = qp * cos_ref[...] + partner * sin_ref[...]
    szb_ref[...] = jax.nn.silu(_dot(xmv, wb3_ref[:, _OFF_ZB:_OFF_ZB + HEADS * DV]))

    zero_halo = jnp.zeros((HALO, D_MODEL), BF16)
    xme_ref[0:HALO] = jnp.where(i > 0, xmp_ref[0], zero_halo)
    xme_ref[HALO:HALO + tile] = xmv
    xme_ref[HALO + tile:] = jnp.where(i < nt - 1, xmn_ref[0], zero_halo)

    def conv_block(j):
        cs = slice(j * cblk, (j + 1) * cblk)
        xe = xme_ref[...]
        hh = _dot(xe, wa4_ref[:, _OFF_H + j * cblk:_OFF_H + (j + 1) * cblk])
        cg = _dot(xe, wa4_ref[:, _OFF_CG + j * cblk:_OFF_CG + (j + 1) * cblk])
        ue_ref[...] = hh * cg
        conv = (cw_ref[0:1, cs] * ue_ref[HALO - 1:HALO - 1 + tile]
                + cw_ref[1:2, cs] * ue_ref[HALO:HALO + tile]
                + cw_ref[2:3, cs] * ue_ref[HALO + 1:HALO + 1 + tile]
                + cb_ref[:, cs])
        bg = _dot(xmv, wa4_ref[:, _OFF_BG + j * cblk:_OFF_BG + (j + 1) * cblk])
        za = _dot(xmv, wa4_ref[:, _OFF_ZA + j * cblk:_OFF_ZA + (j + 1) * cblk])
        ain_ref[:, cs] = (jax.nn.silu(za) * bg * conv).astype(BF16)

    zero_rows = jnp.zeros((DK, LANES), BF16)

    def head_rows(blocks, e):
        return [blk if (r % 2) == e else zero_rows for r, blk in enumerate(blocks)]

    def scores(c, p, slot):
        rows = slice(c * CHUNK, (c + 1) * CHUNK)
        qpb = qr_ref[rows, p * LANES:(p + 1) * LANES].astype(BF16)
        kblk = [kt_ref[0, p * LANES + r * DK:p * LANES + (r + 1) * DK, rows] for r in range(2)]
        for e in range(2):
            h = 2 * p + e
            s = _dot(qpb, jnp.concatenate(head_rows(kblk, e), axis=0)) * dmask_ref[h]
            s_ref[2 * slot + e] = s.astype(BF16)

    def outputs(c, p, slot):
        rows = slice(c * CHUNK, (c + 1) * CHUNK)
        qp = qr_ref[rows, p * LANES:(p + 1) * LANES]
        qfb = jnp.concatenate([(qp * qd_ref[0, p]).astype(BF16),
                               (qp * qd_ref[1, p]).astype(BF16)], axis=1)
        sfp = sf_ref[p].astype(BF16)
        sblk = ([sfp[r * DK:(r + 1) * DK] for r in range(2)]
                + [sb_ref[0, c, p, r * DK:(r + 1) * DK, :] for r in range(2)])
        for e in range(2):
            h = 2 * p + e
            hs = slice(h * DV, (h + 1) * DV)
            states = jnp.concatenate(head_rows(sblk, e), axis=0)
            o = _dot(s_ref[2 * slot + e], v_ref[0, rows, hs]) + _dot(qfb, states)
            mu = jnp.mean(o, axis=-1, keepdims=True)
            d = o - mu
            var = jnp.mean(d * d, axis=-1, keepdims=True)
            rn = d * lax.rsqrt(var + EPS) * gnw_ref[:, hs]
            bin_ref[rows, hs] = (szb_ref[rows, hs] * rn).astype(BF16)
        if p == NPAIR - 1:
            for pp in range(NPAIR):
                sf_ref[pp] = cdp_ref[0, pp] * sf_ref[pp] + kvf_ref[0, c, pp]

    units = [(c, p) for c in range(nchunk) for p in range(NPAIR)]
    per_blk = len(units) // nblk
    for j in range(nblk):
        group = units[j * per_blk:(j + 1) * per_blk]
        for slot, (c, p) in enumerate(group):
            scores(c, p, slot)
        conv_block(j)
        for slot, (c, p) in enumerate(group):
            outputs(c, p, slot)

    for j in range(nblk):
        cs = slice(j * cblk, (j + 1) * cblk)
        pa_ref[:, cs] = jax.nn.sigmoid(
            _dot(xmv, wb3_ref[:, _OFF_GA + j * cblk:_OFF_GA + (j + 1) * cblk]))
        sgb_ref[:, cs] = jax.nn.sigmoid(
            _dot(xmv, wb3_ref[:, _OFF_GB + j * cblk:_OFF_GB + (j + 1) * cblk]))
    pa_ref[...] = pa_ref[...] * _dot(ain_ref[...], wa_ref[...])

    gate = mod_ref[pl.ds(bi, 1), _MOD_GATE:_MOD_GATE + D_MODEL]
    pieces = [slice(r * CHUNK, (r + 1) * CHUNK) for r in range(nchunk)]
    for rs in pieces:
        yb = _dot(bin_ref[rs], wb_ref[...])
        m_ref[rs] = (pa_ref[rs] + sgb_ref[rs] * yb).astype(BF16)
    for rs in pieces:
        z = x_ref[0, rs] + gate * _dot(m_ref[rs], wo_ref[...])
        o_ref[0, rs] = z * _rms_scale(z) * fnw_ref[...]


def _main(x, xm, mod, fnw, w, w_a, w_b, w_out, conv_w, conv_b, gn_w,
          kt, v, sb, kvf, kvf_ctx, cdp, cos, sin, dmask, qd, *, tile, cblk):
    b, l, _ = x.shape
    nt = l // tile
    nchunk = tile // CHUNK
    hb = tile // HALO
    nhb = l // HALO

    def tile_of(s):
        t = jnp.maximum(s - _N_CONV, 0)
        return t // nt, t % nt

    const = lambda *shape: pl.BlockSpec(shape, lambda s: (0,) * len(shape),
                                        pipeline_mode=pl.Buffered(1))
    w_rows = lambda width: pl.BlockSpec((_W_ROWS, width), lambda s: (jnp.minimum(s, _N_CONV - 1), 0))
    tile_rows = lambda width: pl.BlockSpec((1, tile, width), lambda s: (*tile_of(s), 0))
    chunk_state = pl.BlockSpec((1, nchunk, NPAIR, LANES, DV), lambda s: (*tile_of(s), 0, 0, 0))
    rope_rows = pl.BlockSpec((tile, LANES), lambda s: (tile_of(s)[1], 0))
    return pl.pallas_call(
        functools.partial(_main_kernel, tile=tile, nt=nt, cblk=cblk),
        grid=(_N_CONV + b * nt,),
        in_specs=[
            tile_rows(D_MODEL),
            tile_rows(D_MODEL),
            pl.BlockSpec((1, HALO, D_MODEL),
                         lambda s: (tile_of(s)[0], jnp.maximum(tile_of(s)[1] * hb - 1, 0), 0)),
            pl.BlockSpec((1, HALO, D_MODEL),
                         lambda s: (tile_of(s)[0], jnp.minimum((tile_of(s)[1] + 1) * hb, nhb - 1), 0)),
            const(8, 3 * D_MODEL),
            const(1, D_MODEL),
            w_rows(_IN_W), w_rows(D_MODEL), w_rows(D_MODEL), w_rows(D_MODEL),
            const(3, D_MODEL), const(1, D_MODEL), const(1, D_MODEL),
            pl.BlockSpec((1, HEADS * DK, tile), lambda s: (tile_of(s)[0], 0, tile_of(s)[1])),
            tile_rows(HEADS * DV),
            chunk_state, chunk_state,
            pl.BlockSpec((1, 2, NPAIR, LANES, DV), lambda s: (tile_of(s)[0], 0, 0, 0, 0)),
            const(2, NPAIR, LANES, DV),
            rope_rows, rope_rows,
            const(HEADS, CHUNK, CHUNK),
            const(2, NPAIR, CHUNK, LANES),
        ],
        out_specs=tile_rows(D_MODEL),
        out_shape=jax.ShapeDtypeStruct((b, l, D_MODEL), F32),
        scratch_shapes=[
            pltpu.VMEM((D_MODEL, _A_WIDTH), BF16),
            pltpu.VMEM((D_MODEL, HEADS * DK), BF16),
            pltpu.VMEM((D_MODEL, _B_WIDTH), BF16),
            pltpu.VMEM((D_MODEL, D_MODEL), BF16),
            pltpu.VMEM((D_MODEL, D_MODEL), BF16),
            pltpu.VMEM((D_MODEL, D_MODEL), BF16),
            pltpu.VMEM((NPAIR, LANES, DV), F32),
            pltpu.VMEM((tile + 2 * HALO, D_MODEL), BF16),
            pltpu.VMEM((tile + 2 * HALO, cblk), F32),
            pltpu.VMEM((tile, D_MODEL), BF16),
            pltpu.VMEM((tile, D_MODEL), BF16),
            pltpu.VMEM((tile, D_MODEL), BF16),
            pltpu.VMEM((tile, HEADS * DK), F32),
            pltpu.VMEM((tile, HEADS * DV), F32),
            pltpu.VMEM((tile, D_MODEL), F32),
            pltpu.VMEM((tile, D_MODEL), F32),
            pltpu.VMEM((2 * (nchunk * NPAIR // (D_MODEL // cblk)), CHUNK, CHUNK), BF16),
        ],
        compiler_params=pltpu.CompilerParams(
            dimension_semantics=("arbitrary",), vmem_limit_bytes=VMEM_LIMIT),
        name="main",
    )(x, xm, xm, xm, mod, fnw, w, w_a, w_b, w_out, conv_w, conv_b, gn_w,
      kt, v, sb, kvf, kvf_ctx, cdp, cos, sin, dmask, qd)


def _rope_tables(l):
    pos = np.arange(l)
    row = (pos // GRID_W).astype(np.float64)
    col = (pos % GRID_W).astype(np.float64)
    nf = DK // 4
    inv = ROPE_BASE ** (-np.arange(nf, dtype=np.float64) / nf)
    ang = np.concatenate([row[:, None] * inv, col[:, None] * inv], axis=-1)
    cos = np.tile(np.cos(ang), (1, LANES // HALF))
    sin = np.tile(np.sin(ang), (1, LANES // HALF))
    sin = sin * np.where((np.arange(LANES) % DK) < HALF, -1.0, 1.0)
    as_f32 = lambda t: jnp.asarray(t, F32)
    return as_f32(cos), as_f32(sin), as_f32(cos.T * K_SCALE), as_f32(sin.T * K_SCALE)


def kernel(x, c, ctx, c_ctx, norm_w, ada_w, ada_b, w_in, conv_w, conv_b, decay_logit, gn_w,
           w_a, w_b, w_out, final_norm_w):
    b, l, _ = x.shape
    lc = ctx.shape[1]
    assert lc == 2 * CHUNK and l % 1024 == 0 and b < 8
    kv_tile = 1024
    main_tile = 512
    cblk = 256

    mod = _modulation(c, c_ctx.reshape(1, -1), ada_w[0], ada_b[0].reshape(1, -1),
                      norm_w[0].reshape(1, -1))
    dmask, kdt, qd, cdp = _decay_tables(decay_logit[0].astype(F32))

    cos, sin, cos_t, sin_t = _rope_tables(l)

    _, _, _, _, kvf_ctx, sb0 = _kv_states(ctx, mod, w_in[0], None, kdt, cdp, None,
                                          mod_row=b, tile=lc)
    xm, kt, v, sb, kvf, _ = _kv_states(x, mod, w_in[0], (cos_t, sin_t), kdt, cdp, sb0,
                                       mod_row=None, tile=kv_tile)

    return _main(x, xm, mod, final_norm_w.reshape(1, -1), w_in[0], w_a[0], w_b[0], w_out[0],
                 conv_w[0], conv_b[0].reshape(1, -1), gn_w[0].reshape(1, -1),
                 kt, v, sb, kvf, kvf_ctx, cdp, cos, sin, dmask, qd,
                 tile=main_tile, cblk=cblk)
```

```python
import functools

import numpy as np
import jax
import jax.numpy as jnp
from jax import lax
from jax.experimental import pallas as pl
from jax.experimental.pallas import tpu as pltpu

D_MODEL = 1024
HEADS = 8
DK = 64
DV = 128
CHUNK = 128
GRID_W = 64
ROPE_BASE = 10000.0
EPS = 1e-6
NPAIR = HEADS // 2
LANES = 128
HALF = DK // 2
HALO = 16
K_SCALE = DK ** -0.5
VMEM_LIMIT = 58 * 1024 * 1024

F32 = jnp.float32
BF16 = jnp.bfloat16

_Q_COL, _K_COL, _V_COL = 4096, 4608, 5120
_Q_BLOCK = _Q_COL // (HEADS * DK)
_K_BLOCK = _K_COL // (HEADS * DK)
_V_BLOCK = _V_COL // (HEADS * DV)
_A_WIDTH = 4096
_B_WIDTH = 3072
_B_BLOCK = 6144 // _B_WIDTH
_OFF_H, _OFF_BG, _OFF_CG, _OFF_ZA = 0, 1024, 2048, 3072
_OFF_ZB, _OFF_GA, _OFF_GB = 0, 1024, 2048
_MOD_SHIFT, _MOD_SCALE, _MOD_GATE = 0, D_MODEL, 2 * D_MODEL


def _dot(a, b):
    return jnp.dot(a, b, preferred_element_type=F32)


def _log_sigmoid(x):
    return -(jnp.maximum(-x, 0.0) + jnp.log1p(jnp.exp(-jnp.abs(x))))


def _rms_scale(xv):
    return lax.rsqrt(jnp.mean(xv * xv, axis=-1, keepdims=True) + EPS)


def _mod_kernel(c_ref, cctx_ref, w_ref, b_ref, nw_ref, o_ref, cc_ref):
    j = pl.program_id(0)
    nb = c_ref.shape[0]
    cc_ref[...] = jnp.zeros_like(cc_ref)
    cc_ref[0:nb] = c_ref[...]
    cc_ref[nb:nb + 1] = cctx_ref[...]
    s = jax.nn.silu(cc_ref[...])
    val = jnp.dot(s, w_ref[...], preferred_element_type=F32,
                  precision=lax.Precision.HIGHEST) + b_ref[...]
    o_ref[...] = jnp.where(j == 1, nw_ref[...] * (1.0 + val), val)


def _modulation(c, c_ctx, ada_w, ada_b, norm_w):
    nb = c.shape[0]
    return pl.pallas_call(
        _mod_kernel,
        grid=(3,),
        in_specs=[
            pl.BlockSpec((nb, D_MODEL), lambda j: (0, 0)),
            pl.BlockSpec((1, D_MODEL), lambda j: (0, 0)),
            pl.BlockSpec((D_MODEL, D_MODEL), lambda j: (0, j)),
            pl.BlockSpec((1, D_MODEL), lambda j: (0, j)),
            pl.BlockSpec((1, D_MODEL), lambda j: (0, 0)),
        ],
        out_specs=pl.BlockSpec((8, D_MODEL), lambda j: (0, j)),
        out_shape=jax.ShapeDtypeStruct((8, 3 * D_MODEL), F32),
        scratch_shapes=[pltpu.VMEM((8, D_MODEL), F32)],
        name="mod",
    )(c, c_ctx, ada_w, ada_b, norm_w)


_DMASK_SHAPE = (HEADS, CHUNK, CHUNK)
_KDT_SHAPE = (2, NPAIR, LANES, CHUNK)
_QD_SHAPE = (2, NPAIR, CHUNK, LANES)
_CDP_SHAPE = (2, NPAIR, LANES, DV)


def _fill_decay_tables(dl_ref, dmask_ref=None, kdt_ref=None, qd_ref=None, cdp_ref=None):
    c = float(CHUNK)
    ii = lax.broadcasted_iota(jnp.int32, (CHUNK, CHUNK), 0)
    jj = lax.broadcasted_iota(jnp.int32, (CHUNK, CHUNK), 1)
    diff = (ii - jj).astype(F32)
    row = ii.astype(F32)
    col = jj.astype(F32)
    lg = [[_log_sigmoid(jnp.full((1, LANES), dl_ref[d, h], F32)) for h in range(HEADS)]
          for d in range(2)]
    if dmask_ref is not None:
        for h in range(HEADS):
            dmask_ref[h] = jnp.where(
                diff > 0.0, jnp.exp(lg[0][h] * jnp.maximum(diff, 0.0)),
                jnp.where(diff < 0.0, jnp.exp(lg[1][h] * jnp.maximum(-diff, 0.0)), 2.0))
    lane_b = jj[0:1, :] >= DK
    row_b = ii >= DK
    for p in range(NPAIR):
        if qd_ref is not None:
            lgf = jnp.where(lane_b, lg[0][2 * p + 1], lg[0][2 * p])
            lgb = jnp.where(lane_b, lg[1][2 * p + 1], lg[1][2 * p])
            qd_ref[0, p] = jnp.exp(lgf * (row + 1.0))
            qd_ref[1, p] = jnp.exp(lgb * (c - row))
        lrf = jnp.where(row_b, lg[0][2 * p + 1], lg[0][2 * p])
        lrb = jnp.where(row_b, lg[1][2 * p + 1], lg[1][2 * p])
        if kdt_ref is not None:
            kdt_ref[0, p] = jnp.exp(lrf * (c - 1.0 - col))
            kdt_ref[1, p] = jnp.exp(lrb * col)
        if cdp_ref is not None:
            cdp_ref[0, p] = jnp.exp(lrf * c)
            cdp_ref[1, p] = jnp.exp(lrb * c)


def _kv_kernel(x_ref, ctx_ref, mod_ref, dl_ref, wk_ref, wv_ref, cos_ref, sin_ref,
               xm_ref, kt_ref, v_ref, sb_ref, kvf_ref, kvfc_ref,
               wkt_ref, wvb_ref, kdt_ref, cdp_ref, st_ref, *, ctx_row, nchunk):
    bi = pl.program_id(0)
    i = pl.program_id(1)

    @pl.when((bi == 0) & (i == 0))
    def _():
        wkt_ref[...] = wk_ref[...].T.astype(BF16)
        wvb_ref[...] = wv_ref[...].astype(BF16)
        _fill_decay_tables(dl_ref, kdt_ref=kdt_ref, cdp_ref=cdp_ref)

    row = lax.broadcasted_iota(jnp.int32, (2 * LANES, DV), 0)
    is_a = ((row >> 6) & 1) == 0

    span = 2 * CHUNK

    def project(x, mrow, tok):
        shift = mod_ref[pl.ds(mrow, 1), _MOD_SHIFT:_MOD_SHIFT + D_MODEL]
        scale = mod_ref[pl.ds(mrow, 1), _MOD_SCALE:_MOD_SCALE + D_MODEL]
        xm = (x * _rms_scale(x) * scale + shift).astype(BF16)
        kt = lax.dot_general(wkt_ref[...], xm, (((1,), (1,)), ((), ())),
                             preferred_element_type=F32)
        v = _dot(xm, wvb_ref[...]).astype(BF16)
        krs = []
        for p in range(NPAIR):
            kp = kt[p * LANES:(p + 1) * LANES]
            if tok is None:
                kp = kp * K_SCALE
            else:
                swapped = jnp.concatenate(
                    [kp[HALF:DK], kp[:HALF], kp[DK + HALF:], kp[DK:DK + HALF]], axis=0)
                kp = kp * cos_ref[:, tok] + swapped * sin_ref[:, tok]
                kt_ref[0, p * LANES:(p + 1) * LANES, tok] = kp.astype(BF16)
            krs.append(kp)
        if tok is not None:
            xm_ref[0, tok] = xm
            v_ref[0, tok] = v
        return krs, v

    def advance(g, krs, v):
        for c in (1, 0):
            tok = slice(c * CHUNK, (c + 1) * CHUNK)
            for p in range(NPAIR):
                kc = krs[p][:, tok]
                kd = jnp.concatenate([kc * kdt_ref[0, p], kc * kdt_ref[1, p]], axis=0).astype(BF16)
                both = _dot(kd, v[tok, 2 * p * DV:2 * (p + 1) * DV])
                inc = jnp.where(is_a, both[:, :DV], both[:, DV:])
                if g is None:
                    kvfc_ref[0, c, p] = inc[:LANES]
                else:
                    sb_ref[0, 2 * g + c, p] = st_ref[p].astype(BF16)
                    kvf_ref[0, 2 * g + c, p] = inc[:LANES]
                st_ref[p] = cdp_ref[1, p] * st_ref[p] + inc[LANES:]

    @pl.when(i == 0)
    def _():
        st_ref[...] = jnp.zeros_like(st_ref)
        advance(None, *project(ctx_ref[0], ctx_row, None))

    @pl.when(i > 0)
    def _():
        pending = None
        for g in reversed(range(nchunk // 2)):
            tok = slice(g * span, (g + 1) * span)
            done = project(x_ref[0, tok], bi, tok)
            if pending is not None:
                advance(*pending)
            pending = (g,) + done
        advance(*pending)


def _kv_states(x, ctx, mod, dl, w, cos_t, sin_t, *, tile):
    b, l, _ = x.shape
    lc = ctx.shape[1]
    nt = l // tile
    nchunk = tile // CHUNK
    nc = l // CHUNK
    tile_of = lambda i: nt - jnp.maximum(i, 1)
    rows = lambda width: pl.BlockSpec((1, tile, width), lambda bi, i: (bi, tile_of(i), 0))
    const = lambda *shape, **kw: pl.BlockSpec(shape, lambda bi, i: (0,) * len(shape), **kw)
    w_cols = lambda width, blk: pl.BlockSpec((D_MODEL, width), lambda bi, i: (0, blk),
                                             pipeline_mode=pl.Buffered(1))
    rope_cols = pl.BlockSpec((LANES, tile), lambda bi, i: (0, tile_of(i)))
    state_blk = pl.BlockSpec((1, nchunk, NPAIR, LANES, DV), lambda bi, i: (bi, tile_of(i), 0, 0, 0))
    return pl.pallas_call(
        functools.partial(_kv_kernel, ctx_row=b, nchunk=nchunk),
        grid=(b, nt + 1),
        in_specs=[
            rows(D_MODEL),
            pl.BlockSpec((1, lc, D_MODEL), lambda bi, i: (bi, 0, 0)),
            const(8, 3 * D_MODEL),
            pl.BlockSpec(memory_space=pltpu.SMEM),
            w_cols(HEADS * DK, _K_BLOCK), w_cols(HEADS * DV, _V_BLOCK),
            rope_cols, rope_cols,
        ],
        out_specs=(
            rows(D_MODEL),
            pl.BlockSpec((1, HEADS * DK, tile), lambda bi, i: (bi, 0, tile_of(i))),
            rows(HEADS * DV),
            state_blk,
            state_blk,
            pl.BlockSpec((1, lc // CHUNK, NPAIR, LANES, DV), lambda bi, i: (bi, 0, 0, 0, 0)),
        ),
        out_shape=(
            jax.ShapeDtypeStruct((b, l, D_MODEL), BF16),
            jax.ShapeDtypeStruct((b, HEADS * DK, l), BF16),
            jax.ShapeDtypeStruct((b, l, HEADS * DV), BF16),
            jax.ShapeDtypeStruct((b, nc, NPAIR, LANES, DV), BF16),
            jax.ShapeDtypeStruct((b, nc, NPAIR, LANES, DV), F32),
            jax.ShapeDtypeStruct((b, lc // CHUNK, NPAIR, LANES, DV), F32),
        ),
        scratch_shapes=[
            pltpu.VMEM((HEADS * DK, D_MODEL), BF16),
            pltpu.VMEM((D_MODEL, HEADS * DV), BF16),
            pltpu.VMEM(_KDT_SHAPE, F32),
            pltpu.VMEM(_CDP_SHAPE, F32),
            pltpu.VMEM((NPAIR, LANES, DV), F32),
        ],
        compiler_params=pltpu.CompilerParams(
            dimension_semantics=("arbitrary", "arbitrary"), vmem_limit_bytes=VMEM_LIMIT),
        name="kv",
    )(x, ctx, mod, dl, w, w, cos_t, sin_t)


_W_ROWS = 64
_N_CONV = D_MODEL // _W_ROWS
_IN_W = 9216


def _main_kernel(x_ref, xm_ref, xmp_ref, xmn_ref, mod_ref, dl_ref, fnw_ref,
                 win_ref, waf_ref, wbf_ref, wof_ref, cw_ref, cb_ref, gnw_ref,
                 kt_ref, v_ref, sb_ref, kvf_ref, kvfc_ref,
                 cos_ref, sin_ref,
                 o_ref,
                 wa4_ref, wq_ref, wb3_ref, wa_ref, wb_ref, wo_ref, dmask_ref, qd_ref, cdp_ref,
                 *scratch, tile, nt, cblk):
    s = pl.program_id(0)

    @pl.when(s == 0)
    def _():
        _fill_decay_tables(dl_ref, dmask_ref=dmask_ref, qd_ref=qd_ref, cdp_ref=cdp_ref)

    @pl.when(s < _N_CONV)
    def _():
        rows = pl.ds(pl.multiple_of(s * _W_ROWS, _W_ROWS), _W_ROWS)
        wa4_ref[rows, :] = win_ref[:, 0:_A_WIDTH].astype(BF16)
        wq_ref[rows, :] = win_ref[:, _Q_COL:_K_COL].astype(BF16)
        wb3_ref[rows, :] = win_ref[:, _B_BLOCK * _B_WIDTH:(_B_BLOCK + 1) * _B_WIDTH].astype(BF16)
        wa_ref[rows, :] = waf_ref[...].astype(BF16)
        wb_ref[rows, :] = wbf_ref[...].astype(BF16)
        wo_ref[rows, :] = wof_ref[...].astype(BF16)

    @pl.when(s >= _N_CONV)
    def _():
        t = s - _N_CONV
        _tile_step(t // nt, t % nt, x_ref, xm_ref, xmp_ref, xmn_ref, mod_ref, fnw_ref,
                   wa4_ref, wq_ref, wb3_ref, wa_ref, wb_ref, wo_ref, cw_ref, cb_ref, gnw_ref,
                   kt_ref, v_ref, sb_ref, kvf_ref, kvfc_ref, cdp_ref,
                   cos_ref, sin_ref, dmask_ref, qd_ref, o_ref, *scratch,
                   tile=tile, nt=nt, cblk=cblk)


def _tile_step(bi, i, x_ref, xm_ref, xmp_ref, xmn_ref, mod_ref, fnw_ref,
               wa4_ref, wq_ref, wb3_ref, wa_ref, wb_ref, wo_ref, cw_ref, cb_ref, gnw_ref,
               kt_ref, v_ref, sb_ref, kvf_ref, kvfc_ref, cdp_ref,
               cos_ref, sin_ref, dmask_ref, qd_ref,
               o_ref,
               sf_ref, xme_ref, ue_ref, ain_ref, bin_ref, m_ref, qr_ref, szb_ref, pa_ref, sgb_ref,
               s_ref, *, tile, nt, cblk):
    nchunk = tile // CHUNK
    nblk = D_MODEL // cblk

    @pl.when(i == 0)
    def _():
        for p in range(NPAIR):
            sf_ref[p] = cdp_ref[0, p] * kvfc_ref[0, 0, p] + kvfc_ref[0, 1, p]

    xmv = xm_ref[0]

    q = _dot(xmv, wq_ref[...])
    first_half = (lax.broadcasted_iota(jnp.int32, (1, LANES), 1) & HALF) == 0
    for p in range(NPAIR):
        qp = q[:, p * LANES:(p + 1) * LANES]
        partner = jnp.where(first_half, pltpu.roll(qp, LANES - HALF, 1), pltpu.roll(qp, HALF, 1))
        qr_ref[:, p * LANES:(p + 1) * LANES] = qp * cos_ref[...] + partner * sin_ref[...]
    szb_ref[...] = jax.nn.silu(_dot(xmv, wb3_ref[:, _OFF_ZB:_OFF_ZB + HEADS * DV]))

    zero_halo = jnp.zeros((HALO, D_MODEL), BF16)
    xme_ref[0:HALO] = jnp.where(i > 0, xmp_ref[0], zero_halo)
    xme_ref[HALO:HALO + tile] = xmv
    xme_ref[HALO + tile:] = jnp.where(i < nt - 1, xmn_ref[0], zero_halo)

    def conv_block(j):
        cs = slice(j * cblk, (j + 1) * cblk)
        xe = xme_ref[...]
        hh = _dot(xe, wa4_ref[:, _OFF_H + j * cblk:_OFF_H + (j + 1) * cblk])
        cg = _dot(xe, wa4_ref[:, _OFF_CG + j * cblk:_OFF_CG + (j + 1) * cblk])
        ue_ref[...] = hh * cg
        conv = (cw_ref[0:1, cs] * ue_ref[HALO - 1:HALO - 1 + tile]
                + cw_ref[1:2, cs] * ue_ref[HALO:HALO + tile]
                + cw_ref[2:3, cs] * ue_ref[HALO + 1:HALO + 1 + tile]
                + cb_ref[:, cs])
        bg = _dot(xmv, wa4_ref[:, _OFF_BG + j * cblk:_OFF_BG + (j + 1) * cblk])
        za = _dot(xmv, wa4_ref[:, _OFF_ZA + j * cblk:_OFF_ZA + (j + 1) * cblk])
        ain_ref[:, cs] = (jax.nn.silu(za) * bg * conv).astype(BF16)

    zero_rows = jnp.zeros((DK, LANES), BF16)

    def head_rows(blocks, e):
        return [blk if (r % 2) == e else zero_rows for r, blk in enumerate(blocks)]

    def scores(c, p, slot):
        rows = slice(c * CHUNK, (c + 1) * CHUNK)
        qpb = qr_ref[rows, p * LANES:(p + 1) * LANES].astype(BF16)
        kblk = [kt_ref[0, p * LANES + r * DK:p * LANES + (r + 1) * DK, rows] for r in range(2)]
        for e in range(2):
            h = 2 * p + e
            s = _dot(qpb, jnp.concatenate(head_rows(kblk, e), axis=0)) * dmask_ref[h]
            s_ref[2 * slot + e] = s.astype(BF16)

    def outputs(c, p, slot):
        rows = slice(c * CHUNK, (c + 1) * CHUNK)
        qp = qr_ref[rows, p * LANES:(p + 1) * LANES]
        qfb = jnp.concatenate([(qp * qd_ref[0, p]).astype(BF16),
                               (qp * qd_ref[1, p]).astype(BF16)], axis=1)
        sfp = sf_ref[p].astype(BF16)
        sblk = ([sfp[r * DK:(r + 1) * DK] for r in range(2)]
                + [sb_ref[0, c, p, r * DK:(r + 1) * DK, :] for r in range(2)])
        for e in range(2):
            h = 2 * p + e
            hs = slice(h * DV, (h + 1) * DV)
            states = jnp.concatenate(head_rows(sblk, e), axis=0)
            o = _dot(s_ref[2 * slot + e], v_ref[0, rows, hs]) + _dot(qfb, states)
            mu = jnp.mean(o, axis=-1, keepdims=True)
            d = o - mu
            var = jnp.mean(d * d, axis=-1, keepdims=True)
            rn = d * lax.rsqrt(var + EPS) * gnw_ref[:, hs]
            bin_ref[rows, hs] = (szb_ref[rows, hs] * rn).astype(BF16)
        if p == NPAIR - 1:
            for pp in range(NPAIR):
                sf_ref[pp] = cdp_ref[0, pp] * sf_ref[pp] + kvf_ref[0, c, pp]

    units = [(c, p) for c in range(nchunk) for p in range(NPAIR)]
    per_blk = len(units) // nblk
    for j in range(nblk):
        group = units[j * per_blk:(j + 1) * per_blk]
        for slot, (c, p) in enumerate(group):
            scores(c, p, slot)
        conv_block(j)
        for slot, (c, p) in enumerate(group):
            outputs(c, p, slot)

    for j in range(nblk):
        cs = slice(j * cblk, (j + 1) * cblk)
        pa_ref[:, cs] = jax.nn.sigmoid(
            _dot(xmv, wb3_ref[:, _OFF_GA + j * cblk:_OFF_GA + (j + 1) * cblk]))
        sgb_ref[:, cs] = jax.nn.sigmoid(
            _dot(xmv, wb3_ref[:, _OFF_GB + j * cblk:_OFF_GB + (j + 1) * cblk]))
    pa_ref[...] = pa_ref[...] * _dot(ain_ref[...], wa_ref[...])

    gate = mod_ref[pl.ds(bi, 1), _MOD_GATE:_MOD_GATE + D_MODEL]
    pieces = [slice(r * CHUNK, (r + 1) * CHUNK) for r in range(nchunk)]
    for rs in pieces:
        yb = _dot(bin_ref[rs], wb_ref[...])
        m_ref[rs] = (pa_ref[rs] + sgb_ref[rs] * yb).astype(BF16)
    for rs in pieces:
        z = x_ref[0, rs] + gate * _dot(m_ref[rs], wo_ref[...])
        o_ref[0, rs] = z * _rms_scale(z) * fnw_ref[...]


def _main(x, xm, mod, dl, fnw, w, w_a, w_b, w_out, conv_w, conv_b, gn_w,
          kt, v, sb, kvf, kvf_ctx, cos, sin, *, tile, cblk):
    b, l, _ = x.shape
    nt = l // tile
    nchunk = tile // CHUNK
    hb = tile // HALO
    nhb = l // HALO

    def tile_of(s):
        t = jnp.maximum(s - _N_CONV, 0)
        return t // nt, t % nt

    const = lambda *shape: pl.BlockSpec(shape, lambda s: (0,) * len(shape),
                                        pipeline_mode=pl.Buffered(1))
    w_rows = lambda width: pl.BlockSpec((_W_ROWS, width), lambda s: (jnp.minimum(s, _N_CONV - 1), 0))
    tile_rows = lambda width: pl.BlockSpec((1, tile, width), lambda s: (*tile_of(s), 0))
    chunk_state = pl.BlockSpec((1, nchunk, NPAIR, LANES, DV), lambda s: (*tile_of(s), 0, 0, 0))
    rope_rows = pl.BlockSpec((tile, LANES), lambda s: (tile_of(s)[1], 0))
    return pl.pallas_call(
        functools.partial(_main_kernel, tile=tile, nt=nt, cblk=cblk),
        grid=(_N_CONV + b * nt,),
        in_specs=[
            tile_rows(D_MODEL),
            tile_rows(D_MODEL),
            pl.BlockSpec((1, HALO, D_MODEL),
                         lambda s: (tile_of(s)[0], jnp.maximum(tile_of(s)[1] * hb - 1, 0), 0)),
            pl.BlockSpec((1, HALO, D_MODEL),
                         lambda s: (tile_of(s)[0], jnp.minimum((tile_of(s)[1] + 1) * hb, nhb - 1), 0)),
            const(8, 3 * D_MODEL),
            pl.BlockSpec(memory_space=pltpu.SMEM),
            const(1, D_MODEL),
            w_rows(_IN_W), w_rows(D_MODEL), w_rows(D_MODEL), w_rows(D_MODEL),
            const(3, D_MODEL), const(1, D_MODEL), const(1, D_MODEL),
            pl.BlockSpec((1, HEADS * DK, tile), lambda s: (tile_of(s)[0], 0, tile_of(s)[1])),
            tile_rows(HEADS * DV),
            chunk_state, chunk_state,
            pl.BlockSpec((1, 2, NPAIR, LANES, DV), lambda s: (tile_of(s)[0], 0, 0, 0, 0)),
            rope_rows, rope_rows,
        ],
        out_specs=tile_rows(D_MODEL),
        out_shape=jax.ShapeDtypeStruct((b, l, D_MODEL), F32),
        scratch_shapes=[
            pltpu.VMEM((D_MODEL, _A_WIDTH), BF16),
            pltpu.VMEM((D_MODEL, HEADS * DK), BF16),
            pltpu.VMEM((D_MODEL, _B_WIDTH), BF16),
            pltpu.VMEM((D_MODEL, D_MODEL), BF16),
            pltpu.VMEM((D_MODEL, D_MODEL), BF16),
            pltpu.VMEM((D_MODEL, D_MODEL), BF16),
            pltpu.VMEM(_DMASK_SHAPE, F32),
            pltpu.VMEM(_QD_SHAPE, F32),
            pltpu.VMEM(_CDP_SHAPE, F32),
            pltpu.VMEM((NPAIR, LANES, DV), F32),
            pltpu.VMEM((tile + 2 * HALO, D_MODEL), BF16),
            pltpu.VMEM((tile + 2 * HALO, cblk), F32),
            pltpu.VMEM((tile, D_MODEL), BF16),
            pltpu.VMEM((tile, D_MODEL), BF16),
            pltpu.VMEM((tile, D_MODEL), BF16),
            pltpu.VMEM((tile, HEADS * DK), F32),
            pltpu.VMEM((tile, HEADS * DV), F32),
            pltpu.VMEM((tile, D_MODEL), F32),
            pltpu.VMEM((tile, D_MODEL), F32),
            pltpu.VMEM((2 * (nchunk * NPAIR // (D_MODEL // cblk)), CHUNK, CHUNK), BF16),
        ],
        compiler_params=pltpu.CompilerParams(
            dimension_semantics=("arbitrary",), vmem_limit_bytes=VMEM_LIMIT),
        name="main",
    )(x, xm, xm, xm, mod, dl, fnw, w, w_a, w_b, w_out, conv_w, conv_b, gn_w,
      kt, v, sb, kvf, kvf_ctx, cos, sin)


def _rope_tables(l):
    pos = np.arange(l)
    row = (pos // GRID_W).astype(np.float64)
    col = (pos % GRID_W).astype(np.float64)
    nf = DK // 4
    inv = ROPE_BASE ** (-np.arange(nf, dtype=np.float64) / nf)
    ang = np.concatenate([row[:, None] * inv, col[:, None] * inv], axis=-1)
    cos = np.tile(np.cos(ang), (1, LANES // HALF))
    sin = np.tile(np.sin(ang), (1, LANES // HALF))
    sin = sin * np.where((np.arange(LANES) % DK) < HALF, -1.0, 1.0)
    as_f32 = lambda t: jnp.asarray(t, F32)
    return as_f32(cos), as_f32(sin), as_f32(cos.T * K_SCALE), as_f32(sin.T * K_SCALE)


def kernel(x, c, ctx, c_ctx, norm_w, ada_w, ada_b, w_in, conv_w, conv_b, decay_logit, gn_w,
           w_a, w_b, w_out, final_norm_w):
    b, l, _ = x.shape
    lc = ctx.shape[1]
    assert lc == 2 * CHUNK and l % 1024 == 0 and b < 8
    kv_tile = 1024
    main_tile = 512
    cblk = 256

    mod = _modulation(c, c_ctx.reshape(1, -1), ada_w[0], ada_b[0].reshape(1, -1),
                      norm_w[0].reshape(1, -1))
    dl = decay_logit[0].astype(F32)

    cos, sin, cos_t, sin_t = _rope_tables(l)

    xm, kt, v, sb, kvf, kvf_ctx = _kv_states(x, ctx, mod, dl, w_in[0], cos_t, sin_t, tile=kv_tile)

    return _main(x, xm, mod, dl, final_norm_w.reshape(1, -1), w_in[0], w_a[0], w_b[0], w_out[0],
                 conv_w[0], conv_b[0].reshape(1, -1), gn_w[0].reshape(1, -1),
                 kt, v, sb, kvf, kvf_ctx, cos, sin, tile=main_tile, cblk=cblk)
```

```python
import functools

import numpy as np
import jax
import jax.numpy as jnp
from jax import lax
from jax.experimental import pallas as pl
from jax.experimental.pallas import tpu as pltpu

D_MODEL = 1024
HEADS = 8
DK = 64
DV = 128
CHUNK = 128
GRID_W = 64
ROPE_BASE = 10000.0
EPS = 1e-6
NPAIR = HEADS // 2
LANES = 128
HALF = DK // 2
HALO = 16
K_SCALE = DK ** -0.5
VMEM_LIMIT = 58 * 1024 * 1024

F32 = jnp.float32
BF16 = jnp.bfloat16

_Q_COL, _K_COL, _V_COL = 4096, 4608, 5120
_Q_BLOCK = _Q_COL // (HEADS * DK)
_K_BLOCK = _K_COL // (HEADS * DK)
_V_BLOCK = _V_COL // (HEADS * DV)
_A_WIDTH = 4096
_B_WIDTH = 3072
_B_BLOCK = 6144 // _B_WIDTH
_OFF_H, _OFF_BG, _OFF_CG, _OFF_ZA = 0, 1024, 2048, 3072
_OFF_ZB, _OFF_GA, _OFF_GB = 0, 1024, 2048
_MOD_SHIFT, _MOD_SCALE, _MOD_GATE = 0, D_MODEL, 2 * D_MODEL


def _dot(a, b):
    return jnp.dot(a, b, preferred_element_type=F32)


def _log_sigmoid(x):
    return -(jnp.maximum(-x, 0.0) + jnp.log1p(jnp.exp(-jnp.abs(x))))


def _rms_scale(xv):
    return lax.rsqrt(jnp.mean(xv * xv, axis=-1, keepdims=True) + EPS)


def _mod_kernel(c_ref, cctx_ref, w_ref, b_ref, nw_ref, o_ref, cc_ref):
    j = pl.program_id(0)
    nb = c_ref.shape[0]
    cc_ref[...] = jnp.zeros_like(cc_ref)
    cc_ref[0:nb] = c_ref[...]
    cc_ref[nb:nb + 1] = cctx_ref[...]
    def split(a):
        hi = a.astype(BF16)
        return hi, (a - hi.astype(F32)).astype(BF16)

    s_hi, s_lo = split(jax.nn.silu(cc_ref[...]))
    w_hi, w_lo = split(w_ref[...])
    val = _dot(s_hi, w_hi) + (_dot(s_hi, w_lo) + _dot(s_lo, w_hi)) + b_ref[...]
    o_ref[...] = jnp.where(j == 1, nw_ref[...] * (1.0 + val), val)


def _modulation(c, c_ctx, ada_w, ada_b, norm_w):
    nb = c.shape[0]
    return pl.pallas_call(
        _mod_kernel,
        grid=(3,),
        in_specs=[
            pl.BlockSpec((nb, D_MODEL), lambda j: (0, 0)),
            pl.BlockSpec((1, D_MODEL), lambda j: (0, 0)),
            pl.BlockSpec((D_MODEL, D_MODEL), lambda j: (0, j)),
            pl.BlockSpec((1, D_MODEL), lambda j: (0, j)),
            pl.BlockSpec((1, D_MODEL), lambda j: (0, 0)),
        ],
        out_specs=pl.BlockSpec((8, D_MODEL), lambda j: (0, j)),
        out_shape=jax.ShapeDtypeStruct((8, 3 * D_MODEL), F32),
        scratch_shapes=[pltpu.VMEM((8, D_MODEL), F32)],
        name="mod",
    )(c, c_ctx, ada_w, ada_b, norm_w)


_DMASK_SHAPE = (HEADS, CHUNK, CHUNK)
_KDT_SHAPE = (2, NPAIR, LANES, CHUNK)
_QD_SHAPE = (2, NPAIR, CHUNK, LANES)
_CDP_SHAPE = (2, NPAIR, LANES, DV)


def _fill_decay_tables(dl_ref, dmask_ref=None, kdt_ref=None, qd_ref=None, cdp_ref=None):
    c = float(CHUNK)
    ii = lax.broadcasted_iota(jnp.int32, (CHUNK, CHUNK), 0)
    jj = lax.broadcasted_iota(jnp.int32, (CHUNK, CHUNK), 1)
    diff = (ii - jj).astype(F32)
    row = ii.astype(F32)
    col = jj.astype(F32)
    lg = [[_log_sigmoid(jnp.full((1, LANES), dl_ref[d, h], F32)) for h in range(HEADS)]
          for d in range(2)]
    if dmask_ref is not None:
        for h in range(HEADS):
            dmask_ref[h] = jnp.where(
                diff > 0.0, jnp.exp(lg[0][h] * jnp.maximum(diff, 0.0)),
                jnp.where(diff < 0.0, jnp.exp(lg[1][h] * jnp.maximum(-diff, 0.0)), 2.0))
    lane_b = jj[0:1, :] >= DK
    row_b = ii >= DK
    for p in range(NPAIR):
        if qd_ref is not None:
            lgf = jnp.where(lane_b, lg[0][2 * p + 1], lg[0][2 * p])
            lgb = jnp.where(lane_b, lg[1][2 * p + 1], lg[1][2 * p])
            qd_ref[0, p] = jnp.exp(lgf * (row + 1.0))
            qd_ref[1, p] = jnp.exp(lgb * (c - row))
        lrf = jnp.where(row_b, lg[0][2 * p + 1], lg[0][2 * p])
        lrb = jnp.where(row_b, lg[1][2 * p + 1], lg[1][2 * p])
        if kdt_ref is not None:
            kdt_ref[0, p] = jnp.exp(lrf * (c - 1.0 - col))
            kdt_ref[1, p] = jnp.exp(lrb * col)
        if cdp_ref is not None:
            cdp_ref[0, p] = jnp.exp(lrf * c)
            cdp_ref[1, p] = jnp.exp(lrb * c)


def _kv_kernel(x_ref, ctx_ref, mod_ref, dl_ref, wk_ref, wv_ref, cos_ref, sin_ref,
               xm_ref, kt_ref, v_ref, sb_ref, kvf_ref, kvfc_ref,
               wkt_ref, wvb_ref, kdt_ref, cdp_ref, st_ref, *, ctx_row, nchunk):
    bi = pl.program_id(0)
    i = pl.program_id(1)

    @pl.when((bi == 0) & (i == 0))
    def _():
        wkt_ref[...] = wk_ref[...].T.astype(BF16)
        wvb_ref[...] = wv_ref[...].astype(BF16)
        _fill_decay_tables(dl_ref, kdt_ref=kdt_ref, cdp_ref=cdp_ref)

    row = lax.broadcasted_iota(jnp.int32, (2 * LANES, DV), 0)
    is_a = ((row >> 6) & 1) == 0

    span = 2 * CHUNK

    def project(x, mrow, tok):
        shift = mod_ref[pl.ds(mrow, 1), _MOD_SHIFT:_MOD_SHIFT + D_MODEL]
        scale = mod_ref[pl.ds(mrow, 1), _MOD_SCALE:_MOD_SCALE + D_MODEL]
        xm = (x * _rms_scale(x) * scale + shift).astype(BF16)
        kt = lax.dot_general(wkt_ref[...], xm, (((1,), (1,)), ((), ())),
                             preferred_element_type=F32)
        v = _dot(xm, wvb_ref[...]).astype(BF16)
        krs = []
        for p in range(NPAIR):
            kp = kt[p * LANES:(p + 1) * LANES]
            if tok is None:
                kp = kp * K_SCALE
            else:
                swapped = jnp.concatenate(
                    [kp[HALF:DK], kp[:HALF], kp[DK + HALF:], kp[DK:DK + HALF]], axis=0)
                kp = kp * cos_ref[:, tok] + swapped * sin_ref[:, tok]
                kt_ref[0, p * LANES:(p + 1) * LANES, tok] = kp.astype(BF16)
            krs.append(kp)
        if tok is not None:
            xm_ref[0, tok] = xm
            v_ref[0, tok] = v
        return krs, v

    def advance(g, krs, v):
        for c in (1, 0):
            tok = slice(c * CHUNK, (c + 1) * CHUNK)
            for p in range(NPAIR):
                kc = krs[p][:, tok]
                kd = jnp.concatenate([kc * kdt_ref[0, p], kc * kdt_ref[1, p]], axis=0).astype(BF16)
                both = _dot(kd, v[tok, 2 * p * DV:2 * (p + 1) * DV])
                inc = jnp.where(is_a, both[:, :DV], both[:, DV:])
                if g is None:
                    kvfc_ref[0, c, p] = inc[:LANES]
                else:
                    sb_ref[0, 2 * g + c, p] = st_ref[p].astype(BF16)
                    kvf_ref[0, 2 * g + c, p] = inc[:LANES]
                st_ref[p] = cdp_ref[1, p] * st_ref[p] + inc[LANES:]

    @pl.when(i == 0)
    def _():
        st_ref[...] = jnp.zeros_like(st_ref)
        advance(None, *project(ctx_ref[0], ctx_row, None))

    @pl.when(i > 0)
    def _():
        pending = None
        for g in reversed(range(nchunk // 2)):
            tok = slice(g * span, (g + 1) * span)
            done = project(x_ref[0, tok], bi, tok)
            if pending is not None:
                advance(*pending)
            pending = (g,) + done
        advance(*pending)


def _kv_states(x, ctx, mod, dl, w, cos_t, sin_t, *, tile):
    b, l, _ = x.shape
    lc = ctx.shape[1]
    nt = l // tile
    nchunk = tile // CHUNK
    nc = l // CHUNK
    tile_of = lambda i: nt - jnp.maximum(i, 1)
    rows = lambda width: pl.BlockSpec((1, tile, width), lambda bi, i: (bi, tile_of(i), 0))
    const = lambda *shape, **kw: pl.BlockSpec(shape, lambda bi, i: (0,) * len(shape), **kw)
    w_cols = lambda width, blk: pl.BlockSpec((D_MODEL, width), lambda bi, i: (0, blk),
                                             pipeline_mode=pl.Buffered(1))
    rope_cols = pl.BlockSpec((LANES, tile), lambda bi, i: (0, tile_of(i)))
    state_blk = pl.BlockSpec((1, nchunk, NPAIR, LANES, DV), lambda bi, i: (bi, tile_of(i), 0, 0, 0))
    return pl.pallas_call(
        functools.partial(_kv_kernel, ctx_row=b, nchunk=nchunk),
        grid=(b, nt + 1),
        in_specs=[
            rows(D_MODEL),
            pl.BlockSpec((1, lc, D_MODEL), lambda bi, i: (bi, 0, 0)),
            const(8, 3 * D_MODEL),
            pl.BlockSpec(memory_space=pltpu.SMEM),
            w_cols(HEADS * DK, _K_BLOCK), w_cols(HEADS * DV, _V_BLOCK),
            rope_cols, rope_cols,
        ],
        out_specs=(
            rows(D_MODEL),
            pl.BlockSpec((1, HEADS * DK, tile), lambda bi, i: (bi, 0, tile_of(i))),
            rows(HEADS * DV),
            state_blk,
            state_blk,
            pl.BlockSpec((1, lc // CHUNK, NPAIR, LANES, DV), lambda bi, i: (bi, 0, 0, 0, 0)),
        ),
        out_shape=(
            jax.ShapeDtypeStruct((b, l, D_MODEL), BF16),
            jax.ShapeDtypeStruct((b, HEADS * DK, l), BF16),
            jax.ShapeDtypeStruct((b, l, HEADS * DV), BF16),
            jax.ShapeDtypeStruct((b, nc, NPAIR, LANES, DV), BF16),
            jax.ShapeDtypeStruct((b, nc, NPAIR, LANES, DV), F32),
            jax.ShapeDtypeStruct((b, lc // CHUNK, NPAIR, LANES, DV), F32),
        ),
        scratch_shapes=[
            pltpu.VMEM((HEADS * DK, D_MODEL), BF16),
            pltpu.VMEM((D_MODEL, HEADS * DV), BF16),
            pltpu.VMEM(_KDT_SHAPE, F32),
            pltpu.VMEM(_CDP_SHAPE, F32),
            pltpu.VMEM((NPAIR, LANES, DV), F32),
        ],
        compiler_params=pltpu.CompilerParams(
            dimension_semantics=("arbitrary", "arbitrary"), vmem_limit_bytes=VMEM_LIMIT),
        name="kv",
    )(x, ctx, mod, dl, w, w, cos_t, sin_t)


_W_ROWS = 64
_N_CONV = D_MODEL // _W_ROWS
_IN_W = 9216


def _main_kernel(x_ref, xm_ref, xmp_ref, xmn_ref, mod_ref, dl_ref, fnw_ref,
                 win_ref, waf_ref, wbf_ref, wof_ref, cw_ref, cb_ref, gnw_ref,
                 kt_ref, v_ref, sb_ref, kvf_ref, kvfc_ref,
                 cos_ref, sin_ref,
                 o_ref,
                 wa4_ref, wq_ref, wb3_ref, wa_ref, wb_ref, wo_ref, dmask_ref, qd_ref, cdp_ref,
                 *scratch, tile, nt, cblk):
    s = pl.program_id(0)

    @pl.when(s == 0)
    def _():
        _fill_decay_tables(dl_ref, dmask_ref=dmask_ref, qd_ref=qd_ref, cdp_ref=cdp_ref)

    @pl.when(s < _N_CONV)
    def _():
        rows = pl.ds(pl.multiple_of(s * _W_ROWS, _W_ROWS), _W_ROWS)
        wa4_ref[rows, :] = win_ref[:, 0:_A_WIDTH].astype(BF16)
        wq_ref[rows, :] = win_ref[:, _Q_COL:_K_COL].astype(BF16)
        wb3_ref[rows, :] = win_ref[:, _B_BLOCK * _B_WIDTH:(_B_BLOCK + 1) * _B_WIDTH].astype(BF16)
        wa_ref[rows, :] = waf_ref[...].astype(BF16)
        wb_ref[rows, :] = wbf_ref[...].astype(BF16)
        wo_ref[rows, :] = wof_ref[...].astype(BF16)

    @pl.when(s >= _N_CONV)
    def _():
        t = s - _N_CONV
        _tile_step(t // nt, t % nt, x_ref, xm_ref, xmp_ref, xmn_ref, mod_ref, fnw_ref,
                   wa4_ref, wq_ref, wb3_ref, wa_ref, wb_ref, wo_ref, cw_ref, cb_ref, gnw_ref,
                   kt_ref, v_ref, sb_ref, kvf_ref, kvfc_ref, cdp_ref,
                   cos_ref, sin_ref, dmask_ref, qd_ref, o_ref, *scratch,
                   tile=tile, nt=nt, cblk=cblk)


def _tile_step(bi, i, x_ref, xm_ref, xmp_ref, xmn_ref, mod_ref, fnw_ref,
               wa4_ref, wq_ref, wb3_ref, wa_ref, wb_ref, wo_ref, cw_ref, cb_ref, gnw_ref,
               kt_ref, v_ref, sb_ref, kvf_ref, kvfc_ref, cdp_ref,
               cos_ref, sin_ref, dmask_ref, qd_ref,
               o_ref,
               sf_ref, xme_ref, ue_ref, ain_ref, bin_ref, m_ref, qr_ref, szb_ref, pa_ref, sgb_ref,
               s_ref, *, tile, nt, cblk):
    nchunk = tile // CHUNK
    nblk = D_MODEL // cblk

    @pl.when(i == 0)
    def _():
        for p in range(NPAIR):
            sf_ref[p] = cdp_ref[0, p] * kvfc_ref[0, 0, p] + kvfc_ref[0, 1, p]

    xmv = xm_ref[0]

    q = _dot(xmv, wq_ref[...])
    first_half = (lax.broadcasted_iota(jnp.int32, (1, LANES), 1) & HALF) == 0
    for p in range(NPAIR):
        qp = q[:, p * LANES:(p + 1) * LANES]
        partner = jnp.where(first_half, pltpu.roll(qp, LANES - HALF, 1), pltpu.roll(qp, HALF, 1))
        qr_ref[:, p * LANES:(p + 1) * LANES] = qp * cos_ref[...] + partner * sin_ref[...]
    szb_ref[...] = jax.nn.silu(_dot(xmv, wb3_ref[:, _OFF_ZB:_OFF_ZB + HEADS * DV]))

    zero_halo = jnp.zeros((HALO, D_MODEL), BF16)
    xme_ref[0:HALO] = jnp.where(i > 0, xmp_ref[0], zero_halo)
    xme_ref[HALO:HALO + tile] = xmv
    xme_ref[HALO + tile:] = jnp.where(i < nt - 1, xmn_ref[0], zero_halo)

    def conv_block(j):
        cs = slice(j * cblk, (j + 1) * cblk)
        xe = xme_ref[...]
        hh = _dot(xe, wa4_ref[:, _OFF_H + j * cblk:_OFF_H + (j + 1) * cblk])
        cg = _dot(xe, wa4_ref[:, _OFF_CG + j * cblk:_OFF_CG + (j + 1) * cblk])
        ue_ref[...] = hh * cg
        conv = (cw_ref[0:1, cs] * ue_ref[HALO - 1:HALO - 1 + tile]
                + cw_ref[1:2, cs] * ue_ref[HALO:HALO + tile]
                + cw_ref[2:3, cs] * ue_ref[HALO + 1:HALO + 1 + tile]
                + cb_ref[:, cs])
        bg = _dot(xmv, wa4_ref[:, _OFF_BG + j * cblk:_OFF_BG + (j + 1) * cblk])
        za = _dot(xmv, wa4_ref[:, _OFF_ZA + j * cblk:_OFF_ZA + (j + 1) * cblk])
        ain_ref[:, cs] = (jax.nn.silu(za) * bg * conv).astype(BF16)

    zero_rows = jnp.zeros((DK, LANES), BF16)

    def head_rows(blocks, e):
        return [blk if (r % 2) == e else zero_rows for r, blk in enumerate(blocks)]

    def scores(c, p, slot):
        rows = slice(c * CHUNK, (c + 1) * CHUNK)
        qpb = qr_ref[rows, p * LANES:(p + 1) * LANES].astype(BF16)
        kblk = [kt_ref[0, p * LANES + r * DK:p * LANES + (r + 1) * DK, rows] for r in range(2)]
        k_both = jnp.concatenate(
            [jnp.concatenate(head_rows(kblk, e), axis=0) for e in range(2)], axis=1)
        s_both = _dot(qpb, k_both)
        for e in range(2):
            s = s_both[:, e * CHUNK:(e + 1) * CHUNK] * dmask_ref[2 * p + e]
            s_ref[2 * slot + e] = s.astype(BF16)

    def outputs(c, p, slot):
        rows = slice(c * CHUNK, (c + 1) * CHUNK)
        qp = qr_ref[rows, p * LANES:(p + 1) * LANES]
        qfb = jnp.concatenate([(qp * qd_ref[0, p]).astype(BF16),
                               (qp * qd_ref[1, p]).astype(BF16)], axis=1)
        sfp = sf_ref[p].astype(BF16)
        sblk = ([sfp[r * DK:(r + 1) * DK] for r in range(2)]
                + [sb_ref[0, c, p, r * DK:(r + 1) * DK, :] for r in range(2)])
        states = jnp.concatenate(
            [jnp.concatenate(head_rows(sblk, e), axis=0) for e in range(2)], axis=1)
        cross = _dot(qfb, states)
        for e in range(2):
            h = 2 * p + e
            hs = slice(h * DV, (h + 1) * DV)
            o = _dot(s_ref[2 * slot + e], v_ref[0, rows, hs]) + cross[:, e * DV:(e + 1) * DV]
            mu = jnp.mean(o, axis=-1, keepdims=True)
            d = o - mu
            var = jnp.mean(d * d, axis=-1, keepdims=True)
            rn = d * lax.rsqrt(var + EPS) * gnw_ref[:, hs]
            bin_ref[rows, hs] = (szb_ref[rows, hs] * rn).astype(BF16)
        if p == NPAIR - 1:
            for pp in range(NPAIR):
                sf_ref[pp] = cdp_ref[0, pp] * sf_ref[pp] + kvf_ref[0, c, pp]

    units = [(c, p) for c in range(nchunk) for p in range(NPAIR)]
    per_blk = len(units) // nblk
    for j in range(nblk):
        group = units[j * per_blk:(j + 1) * per_blk]
        for slot, (c, p) in enumerate(group):
            scores(c, p, slot)
        conv_block(j)
        for slot, (c, p) in enumerate(group):
            outputs(c, p, slot)

    for j in range(nblk):
        cs = slice(j * cblk, (j + 1) * cblk)
        pa_ref[:, cs] = jax.nn.sigmoid(
            _dot(xmv, wb3_ref[:, _OFF_GA + j * cblk:_OFF_GA + (j + 1) * cblk]))
        sgb_ref[:, cs] = jax.nn.sigmoid(
            _dot(xmv, wb3_ref[:, _OFF_GB + j * cblk:_OFF_GB + (j + 1) * cblk]))
    pa_ref[...] = pa_ref[...] * _dot(ain_ref[...], wa_ref[...])

    gate = mod_ref[pl.ds(bi, 1), _MOD_GATE:_MOD_GATE + D_MODEL]
    pieces = [slice(r * CHUNK, (r + 1) * CHUNK) for r in range(nchunk)]
    for rs in pieces:
        yb = _dot(bin_ref[rs], wb_ref[...])
        m_ref[rs] = (pa_ref[rs] + sgb_ref[rs] * yb).astype(BF16)
    for rs in pieces:
        z = x_ref[0, rs] + gate * _dot(m_ref[rs], wo_ref[...])
        o_ref[0, rs] = z * _rms_scale(z) * fnw_ref[...]


def _main(x, xm, mod, dl, fnw, w, w_a, w_b, w_out, conv_w, conv_b, gn_w,
          kt, v, sb, kvf, kvf_ctx, cos, sin, *, tile, cblk):
    b, l, _ = x.shape
    nt = l // tile
    nchunk = tile // CHUNK
    hb = tile // HALO
    nhb = l // HALO

    def tile_of(s):
        t = jnp.maximum(s - _N_CONV, 0)
        return t // nt, t % nt

    const = lambda *shape: pl.BlockSpec(shape, lambda s: (0,) * len(shape),
                                        pipeline_mode=pl.Buffered(1))
    w_rows = lambda width: pl.BlockSpec((_W_ROWS, width), lambda s: (jnp.minimum(s, _N_CONV - 1), 0))
    tile_rows = lambda width: pl.BlockSpec((1, tile, width), lambda s: (*tile_of(s), 0))
    chunk_state = pl.BlockSpec((1, nchunk, NPAIR, LANES, DV), lambda s: (*tile_of(s), 0, 0, 0))
    rope_rows = pl.BlockSpec((tile, LANES), lambda s: (tile_of(s)[1], 0))
    return pl.pallas_call(
        functools.partial(_main_kernel, tile=tile, nt=nt, cblk=cblk),
        grid=(_N_CONV + b * nt,),
        in_specs=[
            tile_rows(D_MODEL),
            tile_rows(D_MODEL),
            pl.BlockSpec((1, HALO, D_MODEL),
                         lambda s: (tile_of(s)[0], jnp.maximum(tile_of(s)[1] * hb - 1, 0), 0)),
            pl.BlockSpec((1, HALO, D_MODEL),
                         lambda s: (tile_of(s)[0], jnp.minimum((tile_of(s)[1] + 1) * hb, nhb - 1), 0)),
            const(8, 3 * D_MODEL),
            pl.BlockSpec(memory_space=pltpu.SMEM),
            const(1, D_MODEL),
            w_rows(_IN_W), w_rows(D_MODEL), w_rows(D_MODEL), w_rows(D_MODEL),
            const(3, D_MODEL), const(1, D_MODEL), const(1, D_MODEL),
            pl.BlockSpec((1, HEADS * DK, tile), lambda s: (tile_of(s)[0], 0, tile_of(s)[1])),
            tile_rows(HEADS * DV),
            chunk_state, chunk_state,
            pl.BlockSpec((1, 2, NPAIR, LANES, DV), lambda s: (tile_of(s)[0], 0, 0, 0, 0)),
            rope_rows, rope_rows,
        ],
        out_specs=tile_rows(D_MODEL),
        out_shape=jax.ShapeDtypeStruct((b, l, D_MODEL), F32),
        scratch_shapes=[
            pltpu.VMEM((D_MODEL, _A_WIDTH), BF16),
            pltpu.VMEM((D_MODEL, HEADS * DK), BF16),
            pltpu.VMEM((D_MODEL, _B_WIDTH), BF16),
            pltpu.VMEM((D_MODEL, D_MODEL), BF16),
            pltpu.VMEM((D_MODEL, D_MODEL), BF16),
            pltpu.VMEM((D_MODEL, D_MODEL), BF16),
            pltpu.VMEM(_DMASK_SHAPE, F32),
            pltpu.VMEM(_QD_SHAPE, F32),
            pltpu.VMEM(_CDP_SHAPE, F32),
            pltpu.VMEM((NPAIR, LANES, DV), F32),
            pltpu.VMEM((tile + 2 * HALO, D_MODEL), BF16),
            pltpu.VMEM((tile + 2 * HALO, cblk), F32),
            pltpu.VMEM((tile, D_MODEL), BF16),
            pltpu.VMEM((tile, D_MODEL), BF16),
            pltpu.VMEM((tile, D_MODEL), BF16),
            pltpu.VMEM((tile, HEADS * DK), F32),
            pltpu.VMEM((tile, HEADS * DV), F32),
            pltpu.VMEM((tile, D_MODEL), F32),
            pltpu.VMEM((tile, D_MODEL), F32),
            pltpu.VMEM((2 * (nchunk * NPAIR // (D_MODEL // cblk)), CHUNK, CHUNK), BF16),
        ],
        compiler_params=pltpu.CompilerParams(
            dimension_semantics=("arbitrary",), vmem_limit_bytes=VMEM_LIMIT),
        name="main",
    )(x, xm, xm, xm, mod, dl, fnw, w, w_a, w_b, w_out, conv_w, conv_b, gn_w,
      kt, v, sb, kvf, kvf_ctx, cos, sin)


def _rope_tables(l):
    pos = np.arange(l)
    row = (pos // GRID_W).astype(np.float64)
    col = (pos % GRID_W).astype(np.float64)
    nf = DK // 4
    inv = ROPE_BASE ** (-np.arange(nf, dtype=np.float64) / nf)
    ang = np.concatenate([row[:, None] * inv, col[:, None] * inv], axis=-1)
    cos = np.tile(np.cos(ang), (1, LANES // HALF))
    sin = np.tile(np.sin(ang), (1, LANES // HALF))
    sin = sin * np.where((np.arange(LANES) % DK) < HALF, -1.0, 1.0)
    as_f32 = lambda t: jnp.asarray(t, F32)
    return as_f32(cos), as_f32(sin), as_f32(cos.T * K_SCALE), as_f32(sin.T * K_SCALE)


def kernel(x, c, ctx, c_ctx, norm_w, ada_w, ada_b, w_in, conv_w, conv_b, decay_logit, gn_w,
           w_a, w_b, w_out, final_norm_w):
    b, l, _ = x.shape
    lc = ctx.shape[1]
    assert lc == 2 * CHUNK and l % 1024 == 0 and b < 8
    kv_tile = 1024
    main_tile = 512
    cblk = 256

    mod = _modulation(c, c_ctx.reshape(1, -1), ada_w[0], ada_b[0].reshape(1, -1),
                      norm_w[0].reshape(1, -1))
    dl = decay_logit[0].astype(F32)

    cos, sin, cos_t, sin_t = _rope_tables(l)

    xm, kt, v, sb, kvf, kvf_ctx = _kv_states(x, ctx, mod, dl, w_in[0], cos_t, sin_t, tile=kv_tile)

    return _main(x, xm, mod, dl, final_norm_w.reshape(1, -1), w_in[0], w_a[0], w_b[0], w_out[0],
                 conv_w[0], conv_b[0].reshape(1, -1), gn_w[0].reshape(1, -1),
                 kt, v, sb, kvf, kvf_ctx, cos, sin, tile=main_tile, cblk=cblk)
```
